```python
import math
import jax, jax.numpy as jnp
from jax import lax
import numpy as np

D_MODEL = 1024
BATCH = 4
SEQ = 4096
DEPTH = 2
DEC_BATCH = 32
DEC_SEQ = 8
PAST_LEN = 8192
PAGE_SIZE = 128

N_HEADS = 8
HEAD_DIM = D_MODEL // 16
ATTN_W = N_HEADS * HEAD_DIM
ROT_DIM = HEAD_DIM // 4
ROPE_THETA = 500000.0
MOBA_BLOCK = 256
MOBA_TOP_K = 3
Q_CHUNK = 8
CONV_CH = D_MODEL // 2
CONV_K = 31
MEM_LEN = 256
MEM_HEADS = 4
MEM_HEAD_DIM = D_MODEL // 8
MEM_W = MEM_HEADS * MEM_HEAD_DIM
N_BRANCH = 3
BRANCH_W = ATTN_W
SPLIT_SIZES = (ATTN_W, ATTN_W, ATTN_W, ATTN_W,
               CONV_CH, CONV_CH, CONV_CH,
               MEM_W, MEM_W,
               N_BRANCH * D_MODEL)
N_IN = sum(SPLIT_SIZES)
RMS_EPS = 1e-6
LN_EPS = 1e-5

kernel_name = "moba_conformer_memory_hybrid_step"


def rms_norm(x, g):
    xf = x.astype(jnp.float32)
    y = xf * lax.rsqrt(jnp.mean(xf * xf, axis=-1, keepdims=True) + RMS_EPS)
    return (y * g.astype(jnp.float32)).astype(x.dtype)


def layer_norm(x, g, b):
    xf = x.astype(jnp.float32)
    mu = jnp.mean(xf, axis=-1, keepdims=True)
    var = jnp.mean(jnp.square(xf - mu), axis=-1, keepdims=True)
    y = (xf - mu) * lax.rsqrt(var + LN_EPS)
    return (y * g.astype(jnp.float32) + b.astype(jnp.float32)).astype(x.dtype)


def partial_rotary(x, pos):
    half = ROT_DIM // 2
    inv = ROPE_THETA ** (-jnp.arange(0, ROT_DIM, 2, dtype=jnp.float32) / ROT_DIM)
    ang = pos.astype(jnp.float32)[:, None] * inv[None, :]
    cos = jnp.cos(ang)[None, :, None, :]
    sin = jnp.sin(ang)[None, :, None, :]
    xf = x.astype(jnp.float32)
    x1, x2, rest = xf[..., :half], xf[..., half:ROT_DIM], xf[..., ROT_DIM:]
    out = jnp.concatenate([x1 * cos - x2 * sin, x2 * cos + x1 * sin, rest], axis=-1)
    return out.astype(x.dtype)


def moba_attention(q, k, v, q_pos):
    B, S, H, D = q.shape
    T = k.shape[1]
    nb = -(-T // MOBA_BLOCK)
    pad = nb * MOBA_BLOCK - T
    kp = jnp.pad(k, ((0, 0), (0, pad), (0, 0), (0, 0)))
    vp = jnp.pad(v, ((0, 0), (0, pad), (0, 0), (0, 0)))
    kb = kp.reshape(B, nb, MOBA_BLOCK, H, D).transpose(0, 3, 1, 2, 4)
    vb = vp.reshape(B, nb, MOBA_BLOCK, H, D).transpose(0, 3, 1, 2, 4)
    kmean = jnp.mean(kb.astype(jnp.float32), axis=3)
    n_sel = min(MOBA_TOP_K, nb)
    qc = math.gcd(S, Q_CHUNK)
    n_chunks = S // qc
    qh = q.transpose(0, 2, 1, 3).reshape(B, H, n_chunks, qc, D).transpose(2, 0, 1, 3, 4)
    posc = q_pos.reshape(n_chunks, qc)
    scale = D ** -0.5
    gather_blocks = jax.vmap(jax.vmap(lambda blocks, ix: blocks[ix]))
    rank = jnp.arange(n_sel + 1)

    def one_chunk(args):
        qq, pp = args
        own = pp // MOBA_BLOCK
        sc = jnp.einsum('bhqd,bhnd->bhqn', qq.astype(jnp.float32), kmean)
        fully_past = jnp.arange(nb)[None, :] < own[:, None]
        sc = jnp.where(fully_past, sc, -jnp.inf)
        _, top = lax.top_k(sc, n_sel)
        own_b = jnp.broadcast_to(own[:, None], (B, H, qc, 1)).astype(top.dtype)
        idx = jnp.concatenate([top, own_b], axis=-1)
        kg = gather_blocks(kb, idx)
        vg = gather_blocks(vb, idx)
        kpos = idx[..., None] * MOBA_BLOCK + jnp.arange(MOBA_BLOCK)
        sel_ok = jnp.where(rank[None, :] < n_sel, rank[None, :] < own[:, None], True)
        ok = sel_ok[None, None, :, :, None] & (kpos <= pp[None, None, :, None, None])
        logits = jnp.einsum('bhqd,bhqjkd->bhqjk', qq, kg,
                            preferred_element_type=jnp.float32) * scale
        logits = jnp.where(ok, logits, -jnp.inf).reshape(B, H, qc, -1)
        p = jax.nn.softmax(logits, axis=-1).reshape(B, H, qc, n_sel + 1, MOBA_BLOCK)
        return jnp.einsum('bhqjk,bhqjkd->bhqd', p.astype(v.dtype), vg)

    out = lax.map(one_chunk, (qh, posc))
    return out.transpose(1, 0, 3, 2, 4).reshape(B, S, H, D)


def memory_attention(q, mk, mv):
    logits = jnp.einsum('bshd,bmhd->bhsm', q, mk,
                        preferred_element_type=jnp.float32) * (MEM_HEAD_DIM ** -0.5)
    p = jax.nn.softmax(logits, axis=-1).astype(mv.dtype)
    return jnp.einsum('bhsm,bmhd->bshd', p, mv)


def causal_depthwise_conv(u, past, w, b):
    full = jnp.concatenate([past, u], axis=1)
    y = lax.conv_general_dilated(full, w[:, None, :], window_strides=(1,), padding='VALID',
                                 dimension_numbers=('NWC', 'WIO', 'NWC'),
                                 feature_group_count=u.shape[-1])
    return y + b, full[:, -(CONV_K - 1):]


def trunk_layer(x, pos, k_past, v_past, conv_past, mem_k, mem_v,
                g_norm, w_in, conv_w, conv_b, ln_g, ln_b, w_branch, w_out):
    B, S, _ = x.shape
    h = rms_norm(x, g_norm)
    z = h @ w_in
    points = np.cumsum(SPLIT_SIZES)[:-1].tolist()
    q, k, v, a_gate, c_val, c_glu, c_gate, m_q, m_gate, merge = jnp.split(z, points, axis=-1)
    q = partial_rotary(q.reshape(B, S, N_HEADS, HEAD_DIM), pos)
    k = partial_rotary(k.reshape(B, S, N_HEADS, HEAD_DIM), pos)
    v = v.reshape(B, S, N_HEADS, HEAD_DIM)
    k_all = jnp.concatenate([k_past, k], axis=1)
    v_all = jnp.concatenate([v_past, v], axis=1)
    a = moba_attention(q, k_all, v_all, pos).reshape(B, S, ATTN_W) * jax.nn.silu(a_gate)
    u = c_val * jax.nn.sigmoid(c_glu)
    c, conv_state = causal_depthwise_conv(u, conv_past, conv_w, conv_b)
    c = jax.nn.silu(layer_norm(c, ln_g, ln_b)) * jax.nn.silu(c_gate)
    m = memory_attention(m_q.reshape(B, S, MEM_HEADS, MEM_HEAD_DIM), mem_k, mem_v)
    m = m.reshape(B, S, MEM_W) * jax.nn.silu(m_gate)
    branches = jnp.stack([a, c, m], axis=2)
    proj = jnp.einsum('bsnc,ncd->bsnd', branches, w_branch)
    gates = jax.nn.sigmoid(merge.reshape(B, S, N_BRANCH, D_MODEL))
    y = jnp.sum(gates * proj, axis=2) @ w_out
    return x + y, k, v, conv_state


def setup_inputs(seed: int = 0) -> dict:
    key = jax.random.key(seed)
    ks = jax.random.split(key, 24)
    n_pages = PAST_LEN // PAGE_SIZE
    n_phys = (DEC_BATCH * n_pages * 5) // 4
    f32 = jnp.float32
    nrm = lambda k, shape, s=1.0: (jax.random.normal(k, shape, f32) * s)
    page_table = jax.random.permutation(ks[0], n_phys)[:DEC_BATCH * n_pages]
    page_table = page_table.reshape(DEC_BATCH, n_pages).astype(jnp.int32)
    return {
        "x_prompt": nrm(ks[1], (BATCH, SEQ, D_MODEL)),
        "x_sample": nrm(ks[2], (DEC_BATCH, DEC_SEQ, D_MODEL)),
        "cache_k": nrm(ks[3], (DEPTH, n_phys, PAGE_SIZE, N_HEADS, HEAD_DIM)),
        "cache_v": nrm(ks[4], (DEPTH, n_phys, PAGE_SIZE, N_HEADS, HEAD_DIM)),
        "cache_mem_k": nrm(ks[5], (DEPTH, DEC_BATCH, MEM_LEN, MEM_HEADS, MEM_HEAD_DIM)),
        "cache_mem_v": nrm(ks[6], (DEPTH, DEC_BATCH, MEM_LEN, MEM_HEADS, MEM_HEAD_DIM)),
        "state_conv": nrm(ks[7], (DEPTH, DEC_BATCH, CONV_K - 1, CONV_CH), 0.5),
        "page_table": page_table,
        "mem_prompt": nrm(ks[8], (BATCH, MEM_LEN, D_MODEL)),
        "g_norm": 1.0 + nrm(ks[9], (DEPTH, D_MODEL), 0.01),
        "w_in": nrm(ks[10], (DEPTH, D_MODEL, N_IN), D_MODEL ** -0.5),
        "conv_w": nrm(ks[11], (DEPTH, CONV_K, CONV_CH), CONV_K ** -0.5),
        "conv_b": nrm(ks[12], (DEPTH, CONV_CH), 0.01),
        "ln_g": 1.0 + nrm(ks[13], (DEPTH, CONV_CH), 0.01),
        "ln_b": nrm(ks[14], (DEPTH, CONV_CH), 0.01),
        "w_mem_k": nrm(ks[15], (DEPTH, D_MODEL, MEM_W), D_MODEL ** -0.5),
        "w_mem_v": nrm(ks[16], (DEPTH, D_MODEL, MEM_W), D_MODEL ** -0.5),
        "w_branch": nrm(ks[17], (DEPTH, N_BRANCH, BRANCH_W, D_MODEL), BRANCH_W ** -0.5),
        "w_out": nrm(ks[18], (DEPTH, D_MODEL, D_MODEL), D_MODEL ** -0.5),
        "g_final": 1.0 + nrm(ks[19], (D_MODEL,), 0.01),
    }


def reference(x_prompt, x_sample, cache_k, cache_v, cache_mem_k, cache_mem_v, state_conv,
              page_table, mem_prompt, g_norm, w_in, conv_w, conv_b, ln_g, ln_b,
              w_mem_k, w_mem_v, w_branch, w_out, g_final):
    bp, sp, _ = x_prompt.shape
    pos_p = jnp.arange(sp, dtype=jnp.int32)
    empty_kv = jnp.zeros((bp, 0, N_HEADS, HEAD_DIM), x_prompt.dtype)
    zero_conv = jnp.zeros((bp, CONV_K - 1, CONV_CH), x_prompt.dtype)
    xp = x_prompt
    kp_l, vp_l, cp_l, mkp_l, mvp_l = [], [], [], [], []
    for l in range(DEPTH):
        mk = (mem_prompt @ w_mem_k[l]).reshape(bp, MEM_LEN, MEM_HEADS, MEM_HEAD_DIM)
        mv = (mem_prompt @ w_mem_v[l]).reshape(bp, MEM_LEN, MEM_HEADS, MEM_HEAD_DIM)
        xp, k_new, v_new, c_new = trunk_layer(
            xp, pos_p, empty_kv, empty_kv, zero_conv, mk, mv,
            g_norm[l], w_in[l], conv_w[l], conv_b[l], ln_g[l], ln_b[l], w_branch[l], w_out[l])
        kp_l.append(k_new); vp_l.append(v_new); cp_l.append(c_new)
        mkp_l.append(mk); mvp_l.append(mv)
    y_prompt = rms_norm(xp, g_final)

    bs, ss, _ = x_sample.shape
    past_len = page_table.shape[1] * PAGE_SIZE
    pos_s = past_len + jnp.arange(ss, dtype=jnp.int32)
    xs = x_sample
    ks_l, vs_l, cs_l = [], [], []
    for l in range(DEPTH):
        k_past = cache_k[l, page_table].reshape(bs, past_len, N_HEADS, HEAD_DIM)
        v_past = cache_v[l, page_table].reshape(bs, past_len, N_HEADS, HEAD_DIM)
        xs, k_new, v_new, c_new = trunk_layer(
            xs, pos_s, k_past, v_past, state_conv[l], cache_mem_k[l], cache_mem_v[l],
            g_norm[l], w_in[l], conv_w[l], conv_b[l], ln_g[l], ln_b[l], w_branch[l], w_out[l])
        ks_l.append(k_new); vs_l.append(v_new); cs_l.append(c_new)
    y_sample = rms_norm(xs, g_final)

    new_k_prompt = jnp.stack(kp_l)
    new_v_prompt = jnp.stack(vp_l)
    new_conv_prompt = jnp.stack(cp_l)
    new_mem_k_prompt = jnp.stack(mkp_l)
    new_mem_v_prompt = jnp.stack(mvp_l)
    new_k_sample = jnp.stack(ks_l)
    new_v_sample = jnp.stack(vs_l)
    new_conv_sample = jnp.stack(cs_l)
    return (y_prompt, y_sample, new_k_prompt, new_v_prompt, new_conv_prompt,
            new_mem_k_prompt, new_mem_v_prompt, new_k_sample, new_v_sample, new_conv_sample)
```

```python
import functools

import jax
import jax.numpy as jnp
from jax import lax
from jax.experimental import pallas as pl
from jax.experimental.pallas import tpu as pltpu

F32 = jnp.float32
BF16 = jnp.bfloat16

D_MODEL = 1024
N_HEADS = 8
HEAD_DIM = 64
ATTN_W = N_HEADS * HEAD_DIM
ROT_DIM = HEAD_DIM // 4
ROPE_THETA = 500000.0
MOBA_BLOCK = 256
MOBA_TOP_K = 3
CONV_CH = 512
CONV_K = 31
CONV_HALO = CONV_K - 1
MEM_LEN = 256
MEM_HEADS = 4
MEM_HEAD_DIM = 128
MEM_W = MEM_HEADS * MEM_HEAD_DIM
PAGE_SIZE = 128
RMS_EPS = 1e-6
LN_EPS = 1e-5
ATTN_SCALE = HEAD_DIM ** -0.5
MEM_SCALE = MEM_HEAD_DIM ** -0.5

LANES = 128
ROW_TILE = 256
PAGES_PER_STEP = 16
HALO_PAD = 32
VMEM_LIMIT = 56 * 1024 * 1024

_Q, _K, _V, _AG, _CV, _CG, _CGATE, _MQ, _MG, _MERGE = (
    0, 512, 1024, 1536, 2048, 2560, 3072, 3584, 4096, 4608)
N_IN = _MERGE + 3 * D_MODEL

_NT = (((1,), (1,)), ((), ()))


def _rms_norm(x, g):
    return x * lax.rsqrt(jnp.mean(x * x, axis=-1, keepdims=True) + RMS_EPS) * g


def _layer_norm(x, g, b):
    mu = jnp.mean(x, axis=-1, keepdims=True)
    xc = x - mu
    var = jnp.mean(xc * xc, axis=-1, keepdims=True)
    return xc * lax.rsqrt(var + LN_EPS) * g + b


def _rope(xc, cosf, sina, sinb):
    return (xc * cosf + pltpu.roll(xc, LANES - ROT_DIM // 2, 1) * sina
            + pltpu.roll(xc, ROT_DIM // 2, 1) * sinb)


def _rope_tables(pos):
    half = ROT_DIM // 2
    inv = ROPE_THETA ** (-jnp.arange(0, ROT_DIM, 2, dtype=F32) / ROT_DIM)
    ang = pos.astype(F32)[:, None] * inv[None, :]
    cos, sin = jnp.cos(ang), jnp.sin(ang)
    n = pos.shape[0]
    zeros_h = jnp.zeros((n, half), F32)
    rest0 = jnp.zeros((n, HEAD_DIM - ROT_DIM), F32)
    cosf = jnp.concatenate([cos, cos, jnp.ones((n, HEAD_DIM - ROT_DIM), F32)], axis=1)
    sina = jnp.concatenate([-sin, zeros_h, rest0], axis=1)
    sinb = jnp.concatenate([zeros_h, sin, rest0], axis=1)
    rep = LANES // HEAD_DIM
    return tuple(jnp.tile(t, (1, rep)) for t in (cosf, sina, sinb))


def _top_k_select(sc, valid, idx, n, axis=0):
    scm = jnp.where(valid, sc, -jnp.inf)
    rank = jnp.zeros(sc.shape, F32)
    for j in range(n):
        sj = scm[j:j + 1, :] if axis == 0 else scm[:, j:j + 1]
        beats = (sj > scm) | ((sj == scm) & (j < idx))
        rank = rank + jnp.where(beats, 1.0, 0.0)
    return valid & (rank < float(MOBA_TOP_K))


def _merge_out(attn, sa, g0, part, x, wb0, wout):
    a = (attn * sa).astype(BF16)
    pa = jnp.dot(a, wb0, preferred_element_type=F32)
    mix = (g0 * pa + part).astype(BF16)
    return x + jnp.dot(mix, wout, preferred_element_type=F32)


def _mem_proj_kernel(mem_ref, wk_ref, wv_ref, mk_out, mv_out):
    mb = mem_ref[...].astype(BF16)
    mk_out[...] = jnp.dot(mb, wk_ref[...].astype(BF16), preferred_element_type=F32)
    mv_out[...] = jnp.dot(mb, wv_ref[...].astype(BF16), preferred_element_type=F32)


def _mem_proj(mem2d, w_mem_k, w_mem_v):
    depth = w_mem_k.shape[0]
    rows = mem2d.shape[0]
    w_spec = pl.BlockSpec((None, D_MODEL, MEM_W), lambda l: (l, 0, 0))
    o_spec = pl.BlockSpec((None, rows, MEM_W), lambda l: (l, 0, 0))
    return pl.pallas_call(
        _mem_proj_kernel,
        grid=(depth,),
        in_specs=[pl.BlockSpec((rows, D_MODEL), lambda l: (0, 0)), w_spec, w_spec],
        out_specs=[o_spec, o_spec],
        out_shape=[jax.ShapeDtypeStruct((depth, rows, MEM_W), F32)] * 2,
        compiler_params=pltpu.CompilerParams(dimension_semantics=("arbitrary",),
                                             vmem_limit_bytes=VMEM_LIMIT),
        name="mem_proj",
    )(mem2d, w_mem_k, w_mem_v)


def _proj_prompt_kernel(x_ref, g_ref, w_ref, cos_ref, sina_ref, sinb_ref, cw_ref, cb_ref, lng_ref, lnb_ref,
                        mk_ref, mv_ref, wb1_ref, wb2_ref,
                        q_out, kt_out, vt_out, kb_out, vtb_out, km_out, sa_out, g0_out, part_out, cst_out,
                        ubuf, *, tiles_per_seq):
    tm = ROW_TILE
    tin = pl.program_id(0) % tiles_per_seq
    hb = _rms_norm(x_ref[...], g_ref[...]).astype(BF16)

    def seg(a, width):
        return jnp.dot(hb, w_ref[:, a:a + width], preferred_element_type=F32)

    cosf, sina, sinb = cos_ref[...], sina_ref[...], sinb_ref[...]
    zq = seg(_Q, ATTN_W)
    zk = seg(_K, ATTN_W)
    for c in range(ATTN_W // LANES):
        sl = slice(c * LANES, (c + 1) * LANES)
        q_out[:, sl] = _rope(zq[:, sl], cosf, sina, sinb)
        kr = _rope(zk[:, sl], cosf, sina, sinb)
        kt_out[sl, :] = kr.T
        kb_out[:, sl] = kr.astype(BF16)
        km_out[:, sl] = jnp.mean(kr, axis=0, keepdims=True)
    zvt = seg(_V, ATTN_W).T
    vt_out[...] = zvt
    vtb_out[...] = zvt.astype(BF16)
    sa_out[...] = jax.nn.silu(seg(_AG, ATTN_W))

    u = seg(_CV, CONV_CH) * jax.nn.sigmoid(seg(_CG, CONV_CH))

    @pl.when(tin == 0)
    def _():
        ubuf[0:HALO_PAD, :] = jnp.zeros((HALO_PAD, CONV_CH), F32)

    @pl.when(tin != 0)
    def _():
        ubuf[HALO_PAD - CONV_HALO:HALO_PAD, :] = ubuf[tm + HALO_PAD - CONV_HALO:tm + HALO_PAD, :]

    ubuf[HALO_PAD:HALO_PAD + tm, :] = u
    conv = jnp.broadcast_to(cb_ref[...], (tm, CONV_CH))
    for t in range(CONV_K):
        off = HALO_PAD - CONV_HALO + t
        conv = conv + cw_ref[t:t + 1, :] * ubuf[off:off + tm, :]
    cst_out[...] = ubuf[tm + HALO_PAD - CONV_HALO:tm + HALO_PAD, :]
    cbr = jax.nn.silu(_layer_norm(conv, lng_ref[...], lnb_ref[...])) * jax.nn.silu(seg(_CGATE, CONV_CH))

    zmq = seg(_MQ, MEM_W)
    mparts = []
    for hd in range(MEM_HEADS):
        sl = slice(hd * MEM_HEAD_DIM, (hd + 1) * MEM_HEAD_DIM)
        s = lax.dot_general(zmq[:, sl].astype(BF16), mk_ref[:, sl].astype(BF16), _NT,
                            preferred_element_type=F32) * MEM_SCALE
        p = jnp.exp(s - jnp.max(s, axis=-1, keepdims=True))
        o = jnp.dot(p.astype(BF16), mv_ref[:, sl].astype(BF16), preferred_element_type=F32)
        mparts.append(o / jnp.sum(p, axis=-1, keepdims=True))
    mbr = jnp.concatenate(mparts, axis=1) * jax.nn.silu(seg(_MG, MEM_W))

    pc = jnp.dot(cbr.astype(BF16), wb1_ref[...], preferred_element_type=F32)
    pm = jnp.dot(mbr.astype(BF16), wb2_ref[...], preferred_element_type=F32)
    g0_out[...] = jax.nn.sigmoid(seg(_MERGE, D_MODEL))
    part_out[...] = (jax.nn.sigmoid(seg(_MERGE + D_MODEL, D_MODEL)) * pc
                     + jax.nn.sigmoid(seg(_MERGE + 2 * D_MODEL, D_MODEL)) * pm)


def _const_spec(shape, ngrid=1):
    zeros = (0,) * len(shape)
    if ngrid == 1:
        return pl.BlockSpec(shape, lambda i: zeros)
    return pl.BlockSpec(shape, lambda i, j: zeros)


def _proj_prompt(x2d, g, w_in_b, tabs, cw, cb, lng, lnb, mk, mv, wb1, wb2, batch, seq):
    tm = ROW_TILE
    n = batch * seq
    tps = seq // tm
    nt = n // tm
    row = lambda w: pl.BlockSpec((tm, w), lambda t: (t, 0))
    tab = pl.BlockSpec((tm, LANES), lambda t: (t % tps, 0))
    mem = pl.BlockSpec((None, MEM_LEN, MEM_W), lambda t: (t // tps, 0, 0))
    in_specs = [
        row(D_MODEL), _const_spec((1, D_MODEL)),
        pl.BlockSpec((D_MODEL, N_IN), lambda t: (0, 0), pipeline_mode=pl.Buffered(1)),
        tab, tab, tab,
        _const_spec((CONV_K, CONV_CH)), _const_spec((1, CONV_CH)), _const_spec((1, CONV_CH)),
        _const_spec((1, CONV_CH)),
        mem, mem, _const_spec((ATTN_W, D_MODEL)), _const_spec((ATTN_W, D_MODEL)),
    ]
    out_shape = [
        jax.ShapeDtypeStruct((n, ATTN_W), F32),
        jax.ShapeDtypeStruct((batch, ATTN_W, seq), F32),
        jax.ShapeDtypeStruct((batch, ATTN_W, seq), F32),
        jax.ShapeDtypeStruct((nt, tm, ATTN_W), BF16),
        jax.ShapeDtypeStruct((nt, ATTN_W, tm), BF16),
        jax.ShapeDtypeStruct((nt, 1, ATTN_W), F32),
        jax.ShapeDtypeStruct((n, ATTN_W), F32),
        jax.ShapeDtypeStruct((n, D_MODEL), F32),
        jax.ShapeDtypeStruct((n, D_MODEL), F32),
        jax.ShapeDtypeStruct((batch, CONV_HALO, CONV_CH), F32),
    ]
    seq_t = pl.BlockSpec((None, ATTN_W, tm), lambda t: (t // tps, 0, t % tps))
    out_specs = [
        row(ATTN_W), seq_t, seq_t,
        pl.BlockSpec((None, tm, ATTN_W), lambda t: (t, 0, 0)),
        pl.BlockSpec((None, ATTN_W, tm), lambda t: (t, 0, 0)),
        pl.BlockSpec((None, 1, ATTN_W), lambda t: (t, 0, 0)),
        row(ATTN_W), row(D_MODEL), row(D_MODEL),
        pl.BlockSpec((None, CONV_HALO, CONV_CH), lambda t: (t // tps, 0, 0)),
    ]
    return pl.pallas_call(
        functools.partial(_proj_prompt_kernel, tiles_per_seq=tps),
        grid=(nt,),
        in_specs=in_specs,
        out_specs=out_specs,
        out_shape=out_shape,
        scratch_shapes=[pltpu.VMEM((tm + HALO_PAD, CONV_CH), F32)],
        compiler_params=pltpu.CompilerParams(dimension_semantics=("arbitrary",),
                                             vmem_limit_bytes=VMEM_LIMIT),
        name="proj_prompt",
    )(x2d, g, w_in_b, *tabs, cw, cb, lng, lnb, mk, mv, wb1, wb2)


def _attn_prompt_kernel(q_ref, kb_ref, vt_ref, km_ref, sa_ref, g0_ref, part_ref, x_ref, wb0_ref, wout_ref,
                        gf_ref, o_ref, sel_scr, acc_scr, *, final, nblk):
    tq = ROW_TILE
    i = pl.program_id(1)
    blk_idx = lax.broadcasted_iota(jnp.int32, (nblk, tq), 0)
    valid = blk_idx < i
    lane = lax.broadcasted_iota(jnp.int32, (tq, LANES), 1)
    causal = (lax.broadcasted_iota(jnp.int32, (MOBA_BLOCK, tq), 0)
              <= lax.broadcasted_iota(jnp.int32, (MOBA_BLOCK, tq), 1))

    for hp in range(N_HEADS // 2):
        sl = slice(hp * LANES, (hp + 1) * LANES)
        qp = q_ref[:, sl]
        kmp = km_ref[:, sl]
        for hh in range(2):
            h = 2 * hp + hh
            rows = slice(h * HEAD_DIM, (h + 1) * HEAD_DIM)
            qm = jnp.where((lane // HEAD_DIM) == hh, qp, 0.0)
            sc = lax.dot_general(kmp, qm, _NT, precision=lax.Precision.HIGHEST,
                                 preferred_element_type=F32)
            sel = _top_k_select(sc, valid, blk_idx, nblk)
            sel_scr[...] = jnp.where(sel, 1.0, 0.0)
            qmb = (qm * ATTN_SCALE).astype(BF16)

            s = lax.dot_general(kb_ref[i, :, sl], qmb, _NT, preferred_element_type=F32)
            s = jnp.where(causal, s, -jnp.inf)
            m0 = jnp.max(s, axis=0, keepdims=True)
            p = jnp.exp(s - m0)
            l0 = jnp.sum(p, axis=0, keepdims=True)
            a0 = jnp.dot(vt_ref[i, rows, :], p.astype(BF16), preferred_element_type=F32)

            def body(j, carry):
                m, l, acc = carry
                s = lax.dot_general(kb_ref[j, :, sl], qmb, _NT, preferred_element_type=F32)
                s = jnp.where(sel_scr[pl.ds(j, 1), :] > 0.0, s, -jnp.inf)
                m_new = jnp.maximum(m, jnp.max(s, axis=0, keepdims=True))
                alpha = jnp.exp(m - m_new)
                p = jnp.exp(s - m_new)
                l = alpha * l + jnp.sum(p, axis=0, keepdims=True)
                acc = alpha * acc + jnp.dot(vt_ref[j, rows, :], p.astype(BF16), preferred_element_type=F32)
                return m_new, l, acc

            _, l, acc = lax.fori_loop(0, i, body, (m0, l0, a0))
            acc_scr[rows, :] = acc / l

    attn = acc_scr[...].T
    xn = _merge_out(attn, sa_ref[...], g0_ref[...], part_ref[...], x_ref[...], wb0_ref[...], wout_ref[...])
    if final:
        xn = _rms_norm(xn, gf_ref[...])
    o_ref[...] = xn


def _attn_prompt(q, kb, vt, km, sa, g0, part, x2d, wb0, wout, gf, batch, seq, final):
    tq = ROW_TILE
    nblk = seq // MOBA_BLOCK
    row = lambda w: pl.BlockSpec((tq, w), lambda b, i: (b * nblk + i, 0))
    in_specs = [
        row(ATTN_W),
        pl.BlockSpec((nblk, MOBA_BLOCK, ATTN_W), lambda b, i: (b, 0, 0)),
        pl.BlockSpec((nblk, ATTN_W, MOBA_BLOCK), lambda b, i: (b, 0, 0)),
        pl.BlockSpec((None, nblk, ATTN_W), lambda b, i: (b, 0, 0)),
        row(ATTN_W), row(D_MODEL), row(D_MODEL), row(D_MODEL),
        _const_spec((ATTN_W, D_MODEL), 2), _const_spec((D_MODEL, D_MODEL), 2), _const_spec((1, D_MODEL), 2),
    ]
    return pl.pallas_call(
        functools.partial(_attn_prompt_kernel, final=final, nblk=nblk),
        grid=(batch, nblk),
        in_specs=in_specs,
        out_specs=row(D_MODEL),
        out_shape=jax.ShapeDtypeStruct((batch * seq, D_MODEL), F32),
        scratch_shapes=[pltpu.VMEM((nblk, tq), F32), pltpu.VMEM((ATTN_W, tq), F32)],
        compiler_params=pltpu.CompilerParams(dimension_semantics=("arbitrary", "arbitrary"),
                                             vmem_limit_bytes=VMEM_LIMIT),
        name="attn_prompt",
    )(q, kb, vt, km, sa, g0, part, x2d, wb0, wout, gf)


def _proj_sample_kernel(x_ref, g_ref, w_ref, cos_ref, sina_ref, sinb_ref, cw_ref, cb_ref, lng_ref, lnb_ref,
                        st_ref, mk_ref, mv_ref, wb1_ref, wb2_ref,
                        q_out, k_out, v_out, sa_out, g0_out, part_out, cst_out,
                        u_scr, cgate_scr, mq_scr, mgate_scr, g12_scr, c_scr, m_scr, full_scr, *, dec_seq):
    b = pl.program_id(0)
    nb = pl.num_programs(0)

    @pl.when(b == 0)
    def _():
        hb = _rms_norm(x_ref[...], g_ref[...]).astype(BF16)

        def seg(a, width):
            return jnp.dot(hb, w_ref[:, a:a + width], preferred_element_type=F32)

        cosf, sina, sinb = cos_ref[...], sina_ref[...], sinb_ref[...]
        zq = seg(_Q, ATTN_W)
        zk = seg(_K, ATTN_W)
        for c in range(ATTN_W // LANES):
            sl = slice(c * LANES, (c + 1) * LANES)
            q_out[:, sl] = _rope(zq[:, sl], cosf, sina, sinb)
            k_out[:, sl] = _rope(zk[:, sl], cosf, sina, sinb)
        v_out[...] = seg(_V, ATTN_W)
        sa_out[...] = jax.nn.silu(seg(_AG, ATTN_W))
        u_scr[...] = seg(_CV, CONV_CH) * jax.nn.sigmoid(seg(_CG, CONV_CH))
        cgate_scr[...] = jax.nn.silu(seg(_CGATE, CONV_CH))
        mq_scr[...] = seg(_MQ, MEM_W)
        mgate_scr[...] = jax.nn.silu(seg(_MG, MEM_W))
        g0_out[...] = jax.nn.sigmoid(seg(_MERGE, D_MODEL))
        g12_scr[:, 0:D_MODEL] = jax.nn.sigmoid(seg(_MERGE + D_MODEL, D_MODEL))
        g12_scr[:, D_MODEL:2 * D_MODEL] = jax.nn.sigmoid(seg(_MERGE + 2 * D_MODEL, D_MODEL))

    r0 = pl.multiple_of(b * dec_seq, dec_seq)
    rows = pl.ds(r0, dec_seq)

    full_scr[0:CONV_HALO, :] = st_ref[...]
    full_scr[CONV_HALO:CONV_HALO + dec_seq, :] = u_scr[rows, :]
    conv = jnp.broadcast_to(cb_ref[...], (dec_seq, CONV_CH))
    for t in range(CONV_K):
        conv = conv + cw_ref[t:t + 1, :] * full_scr[t:t + dec_seq, :]
    cst_out[...] = full_scr[dec_seq:dec_seq + CONV_HALO, :]
    c_scr[rows, :] = jax.nn.silu(_layer_norm(conv, lng_ref[...], lnb_ref[...])) * cgate_scr[rows, :]

    mq = mq_scr[rows, :]
    mparts = []
    for hd in range(MEM_HEADS):
        sl = slice(hd * MEM_HEAD_DIM, (hd + 1) * MEM_HEAD_DIM)
        s = lax.dot_general(mq[:, sl], mk_ref[:, sl], _NT, preferred_element_type=F32) * MEM_SCALE
        p = jnp.exp(s - jnp.max(s, axis=-1, keepdims=True))
        o = jnp.dot(p, mv_ref[:, sl], preferred_element_type=F32)
        mparts.append(o / jnp.sum(p, axis=-1, keepdims=True))
    m_scr[rows, :] = jnp.concatenate(mparts, axis=1) * mgate_scr[rows, :]

    @pl.when(b == nb - 1)
    def _():
        pc = jnp.dot(c_scr[...].astype(BF16), wb1_ref[...], preferred_element_type=F32)
        pm = jnp.dot(m_scr[...].astype(BF16), wb2_ref[...], preferred_element_type=F32)
        part_out[...] = g12_scr[:, 0:D_MODEL] * pc + g12_scr[:, D_MODEL:2 * D_MODEL] * pm


def _proj_sample(x2d, g, w_in_b, tabs, cw, cb, lng, lnb, state_conv, cache_mem_k, cache_mem_v, wb1, wb2,
                 layer, dec_batch, dec_seq):
    n = dec_batch * dec_seq
    full = lambda w: _const_spec((n, w))
    mem = pl.BlockSpec((None, None, MEM_LEN, MEM_W), lambda b: (layer, b, 0, 0))
    in_specs = [
        full(D_MODEL), _const_spec((1, D_MODEL)),
        pl.BlockSpec((D_MODEL, N_IN), lambda b: (0, 0), pipeline_mode=pl.Buffered(1)),
        full(LANES), full(LANES), full(LANES),
        _const_spec((CONV_K, CONV_CH)), _const_spec((1, CONV_CH)), _const_spec((1, CONV_CH)),
        _const_spec((1, CONV_CH)),
        pl.BlockSpec((None, None, CONV_HALO, CONV_CH), lambda b: (layer, b, 0, 0)),
        mem, mem, _const_spec((ATTN_W, D_MODEL)), _const_spec((ATTN_W, D_MODEL)),
    ]
    out_shape = [jax.ShapeDtypeStruct((n, ATTN_W), F32)] * 4 + [
        jax.ShapeDtypeStruct((n, D_MODEL), F32), jax.ShapeDtypeStruct((n, D_MODEL), F32),
        jax.ShapeDtypeStruct((dec_batch, CONV_HALO, CONV_CH), F32)]
    out_specs = [full(ATTN_W)] * 4 + [full(D_MODEL), full(D_MODEL),
                                      pl.BlockSpec((None, CONV_HALO, CONV_CH), lambda b: (b, 0, 0))]
    scr = lambda w: pltpu.VMEM((n, w), F32)
    return pl.pallas_call(
        functools.partial(_proj_sample_kernel, dec_seq=dec_seq),
        grid=(dec_batch,),
        in_specs=in_specs,
        out_specs=out_specs,
        out_shape=out_shape,
        scratch_shapes=[scr(CONV_CH), scr(CONV_CH), scr(MEM_W), scr(MEM_W), scr(2 * D_MODEL),
                        scr(CONV_CH), scr(MEM_W),
                        pltpu.VMEM((CONV_HALO + dec_seq + 2, CONV_CH), F32)],
        compiler_params=pltpu.CompilerParams(dimension_semantics=("arbitrary",),
                                             vmem_limit_bytes=VMEM_LIMIT),
        name="proj_sample",
    )(x2d, g, w_in_b, *tabs, cw, cb, lng, lnb, state_conv, cache_mem_k, cache_mem_v, wb1, wb2)


def _page_index(layer, r, b, c, pt_ref):
    return (layer, pt_ref[b, c * PAGES_PER_STEP + r], 0, 0, 0)


def _head_rows(x, dec_seq):
    rows = N_HEADS * dec_seq
    tiled = jnp.concatenate([x] * N_HEADS, axis=0)
    row_h = lax.broadcasted_iota(jnp.int32, (rows, ATTN_W), 0) // dec_seq
    lane_h = lax.broadcasted_iota(jnp.int32, (rows, ATTN_W), 1) // HEAD_DIM
    return jnp.where(row_h == lane_h, tiled, 0.0)


def _sample_logits_kernel(pt_ref, q_ref, kn_ref, *refs, dec_seq, n_pages):
    kp_refs = refs[:PAGES_PER_STEP]
    p_out, l_out = refs[PAGES_PER_STEP:PAGES_PER_STEP + 2]
    logit_scr, qb_scr = refs[PAGES_PER_STEP + 2:]
    c = pl.program_id(1)
    nc = pl.num_programs(1)
    rows = N_HEADS * dec_seq
    pages_per_blk = MOBA_BLOCK // PAGE_SIZE
    nblk = n_pages // pages_per_blk

    @pl.when(c == 0)
    def _():
        qb_scr[...] = (_head_rows(q_ref[...], dec_seq) * ATTN_SCALE).astype(BF16)

    qb = qb_scr[...]
    for r in range(PAGES_PER_STEP):
        kt = kp_refs[r][...].reshape(ATTN_W, PAGE_SIZE)
        logit_scr[c * PAGES_PER_STEP + r] = jnp.dot(qb, kt.astype(BF16), preferred_element_type=F32)

    @pl.when(c == nc - 1)
    def _():
        blk_idx = lax.broadcasted_iota(jnp.int32, (rows, nblk), 1)
        sc = jnp.zeros((rows, nblk), F32)
        for j in range(nblk):
            blk = logit_scr[j * pages_per_blk]
            for pp in range(1, pages_per_blk):
                blk = blk + logit_scr[j * pages_per_blk + pp]
            sc = jnp.where(blk_idx == j, jnp.sum(blk, axis=-1, keepdims=True), sc)
        sel = _top_k_select(sc, blk_idx >= 0, blk_idx, nblk, axis=1)
        sel_t = jnp.where(sel, 1.0, 0.0)

        kn_page = jnp.concatenate([kn_ref[...], jnp.zeros((PAGE_SIZE - dec_seq, ATTN_W), F32)], axis=0)
        ln = lax.dot_general(qb, kn_page.astype(BF16), _NT, preferred_element_type=F32)
        key_i = lax.broadcasted_iota(jnp.int32, (rows, PAGE_SIZE), 1)
        qry_i = lax.broadcasted_iota(jnp.int32, (rows, PAGE_SIZE), 0) % dec_seq
        ln = jnp.where(key_i <= qry_i, ln, -jnp.inf)

        mx = ln
        for p in range(n_pages):
            j = p // pages_per_blk
            lp = jnp.where(sel_t[:, j:j + 1] > 0.0, logit_scr[p], -jnp.inf)
            logit_scr[p] = lp
            mx = jnp.maximum(mx, lp)
        m = jnp.max(mx, axis=-1, keepdims=True)
        pn = jnp.exp(ln - m)
        p_out[n_pages] = pn.astype(BF16)
        lsum = pn
        for p in range(n_pages):
            pp = jnp.exp(logit_scr[p] - m)
            p_out[p] = pp.astype(BF16)
            lsum = lsum + pp
        l_out[...] = jnp.broadcast_to(jnp.sum(lsum, axis=-1, keepdims=True), (rows, LANES))


def _sample_values_kernel(pt_ref, p_ref, pn_ref, l_ref, vn_ref, *refs, dec_seq):
    vp_refs = refs[:PAGES_PER_STEP]
    o_ref = refs[PAGES_PER_STEP]
    acc_scr = refs[PAGES_PER_STEP + 1]
    c = pl.program_id(1)
    nc = pl.num_programs(1)

    @pl.when(c == 0)
    def _():
        vn_page = jnp.concatenate([vn_ref[...], jnp.zeros((PAGE_SIZE - dec_seq, ATTN_W), F32)], axis=0)
        acc_scr[...] = jnp.dot(pn_ref[...], vn_page.astype(BF16), preferred_element_type=F32)

    acc = acc_scr[...]
    for r in range(PAGES_PER_STEP):
        vt = vp_refs[r][...].reshape(ATTN_W, PAGE_SIZE)
        acc = acc + lax.dot_general(p_ref[r], vt.astype(BF16), _NT, preferred_element_type=F32)
    acc_scr[...] = acc

    @pl.when(c == nc - 1)
    def _():
        res = acc / jnp.concatenate([l_ref[...]] * (ATTN_W // LANES), axis=1)
        lane_h = lax.broadcasted_iota(jnp.int32, (dec_seq, ATTN_W), 1) // HEAD_DIM
        out = jnp.zeros((dec_seq, ATTN_W), F32)
        for h in range(N_HEADS):
            out = out + jnp.where(lane_h == h, res[h * dec_seq:(h + 1) * dec_seq, :], 0.0)
        o_ref[...] = out


def _attn_sample(page_table, q, k_new, v_new, cache_k, cache_v, layer, dec_batch, dec_seq):
    n_pages = page_table.shape[1]
    nc = n_pages // PAGES_PER_STEP
    rows = N_HEADS * dec_seq
    page_specs = [pl.BlockSpec((None, None, N_HEADS, HEAD_DIM, PAGE_SIZE),
                               functools.partial(_page_index, layer, r))
                  for r in range(PAGES_PER_STEP)]
    seq_rows = pl.BlockSpec((dec_seq, ATTN_W), lambda b, c, pt: (b, 0))
    cparams = pltpu.CompilerParams(dimension_semantics=("arbitrary", "arbitrary"),
                                   vmem_limit_bytes=VMEM_LIMIT)

    p_all, l_all = pl.pallas_call(
        functools.partial(_sample_logits_kernel, dec_seq=dec_seq, n_pages=n_pages),
        grid_spec=pltpu.PrefetchScalarGridSpec(
            num_scalar_prefetch=1,
            grid=(dec_batch, nc),
            in_specs=[seq_rows, seq_rows] + page_specs,
            out_specs=[pl.BlockSpec((None, n_pages + 1, rows, PAGE_SIZE), lambda b, c, pt: (b, 0, 0, 0)),
                       pl.BlockSpec((None, rows, LANES), lambda b, c, pt: (b, 0, 0))],
            scratch_shapes=[pltpu.VMEM((n_pages, rows, PAGE_SIZE), F32),
                            pltpu.VMEM((rows, ATTN_W), BF16)]),
        out_shape=[jax.ShapeDtypeStruct((dec_batch, n_pages + 1, rows, PAGE_SIZE), BF16),
                   jax.ShapeDtypeStruct((dec_batch, rows, LANES), F32)],
        compiler_params=cparams,
        name="sample_logits",
    )(page_table, q, k_new, *([cache_k] * PAGES_PER_STEP))

    return pl.pallas_call(
        functools.partial(_sample_values_kernel, dec_seq=dec_seq),
        grid_spec=pltpu.PrefetchScalarGridSpec(
            num_scalar_prefetch=1,
            grid=(dec_batch, nc),
            in_specs=[pl.BlockSpec((None, PAGES_PER_STEP, rows, PAGE_SIZE), lambda b, c, pt: (b, c, 0, 0)),
                      pl.BlockSpec((None, None, rows, PAGE_SIZE), lambda b, c, pt: (b, n_pages, 0, 0)),
                      pl.BlockSpec((None, rows, LANES), lambda b, c, pt: (b, 0, 0)),
                      seq_rows] + page_specs,
            out_specs=seq_rows,
            scratch_shapes=[pltpu.VMEM((rows, ATTN_W), F32)]),
        out_shape=jax.ShapeDtypeStruct((dec_batch * dec_seq, ATTN_W), F32),
        compiler_params=cparams,
        name="sample_values",
    )(page_table, p_all, p_all, l_all, v_new, *([cache_v] * PAGES_PER_STEP))


def _merge_sample_kernel(a_ref, sa_ref, g0_ref, part_ref, x_ref, wb0_ref, wout_ref, gf_ref, o_ref, *, final):
    xn = _merge_out(a_ref[...], sa_ref[...], g0_ref[...], part_ref[...], x_ref[...], wb0_ref[...],
                    wout_ref[...])
    if final:
        xn = _rms_norm(xn, gf_ref[...])
    o_ref[...] = xn


def _merge_sample(attn, sa, g0, part, x2d, wb0, wout, gf, final):
    n = x2d.shape[0]
    full = lambda w: _const_spec((n, w))
    return pl.pallas_call(
        functools.partial(_merge_sample_kernel, final=final),
        grid=(1,),
        in_specs=[full(ATTN_W), full(ATTN_W), full(D_MODEL), full(D_MODEL), full(D_MODEL),
                  _const_spec((ATTN_W, D_MODEL)), _const_spec((D_MODEL, D_MODEL)), _const_spec((1, D_MODEL))],
        out_specs=full(D_MODEL),
        out_shape=jax.ShapeDtypeStruct((n, D_MODEL), F32),
        compiler_params=pltpu.CompilerParams(dimension_semantics=("arbitrary",),
                                             vmem_limit_bytes=VMEM_LIMIT),
        name="merge_sample",
    )(attn, sa, g0, part, x2d, wb0, wout, gf)


def kernel(x_prompt, x_sample, cache_k, cache_v, cache_mem_k, cache_mem_v, state_conv, page_table, mem_prompt,
           g_norm, w_in, conv_w, conv_b, ln_g, ln_b, w_mem_k, w_mem_v, w_branch, w_out, g_final):
    batch, seq, _ = x_prompt.shape
    dec_batch, dec_seq, _ = x_sample.shape
    depth = w_in.shape[0]
    n_phys = cache_k.shape[1]
    past_len = page_table.shape[1] * PAGE_SIZE
    assert seq % ROW_TILE == 0 and ROW_TILE == MOBA_BLOCK
    assert dec_batch * dec_seq == ROW_TILE and dec_seq == 8 and past_len % MOBA_BLOCK == 0
    assert page_table.shape[1] % PAGES_PER_STEP == 0

    w_in_b = w_in.astype(BF16)
    w_br_b = w_branch.astype(BF16)
    w_out_b = w_out.astype(BF16)
    gf = g_final.reshape(1, D_MODEL)

    mem_k_p, mem_v_p = _mem_proj(mem_prompt.reshape(batch * MEM_LEN, D_MODEL), w_mem_k, w_mem_v)

    tabs_p = _rope_tables(jnp.arange(seq, dtype=jnp.int32))
    tabs_s = tuple(jnp.tile(t, (dec_batch, 1))
                   for t in _rope_tables(past_len + jnp.arange(dec_seq, dtype=jnp.int32)))

    cache_kt = cache_k.transpose(0, 1, 3, 4, 2)
    cache_vt = cache_v.transpose(0, 1, 3, 4, 2)
    cmk = cache_mem_k.reshape(depth, dec_batch, MEM_LEN, MEM_W)
    cmv = cache_mem_v.reshape(depth, dec_batch, MEM_LEN, MEM_W)

    xp = x_prompt.reshape(batch * seq, D_MODEL)
    xs = x_sample.reshape(dec_batch * dec_seq, D_MODEL)
    kp_l, vp_l, cp_l, ks_l, vs_l, cs_l = [], [], [], [], [], []
    for l in range(depth):
        g = g_norm[l].reshape(1, D_MODEL)
        cw, cb = conv_w[l], conv_b[l].reshape(1, CONV_CH)
        lng, lnb = ln_g[l].reshape(1, CONV_CH), ln_b[l].reshape(1, CONV_CH)
        final = l == depth - 1

        q, kt, vt, kb, vtb, km, sa, g0, part, cst = _proj_prompt(
            xp, g, w_in_b[l], tabs_p, cw, cb, lng, lnb,
            mem_k_p[l].reshape(batch, MEM_LEN, MEM_W), mem_v_p[l].reshape(batch, MEM_LEN, MEM_W),
            w_br_b[l, 1], w_br_b[l, 2], batch, seq)
        xp = _attn_prompt(q, kb, vtb, km.reshape(batch, seq // MOBA_BLOCK, ATTN_W), sa, g0, part, xp,
                          w_br_b[l, 0], w_out_b[l], gf, batch, seq, final)
        kp_l.append(kt); vp_l.append(vt); cp_l.append(cst)

        qs, k_s, v_s, sas, g0s, parts, csts = _proj_sample(
            xs, g, w_in_b[l], tabs_s, cw, cb, lng, lnb, state_conv, cmk, cmv,
            w_br_b[l, 1], w_br_b[l, 2], l, dec_batch, dec_seq)
        attn_s = _attn_sample(page_table, qs, k_s, v_s, cache_kt, cache_vt, l, dec_batch, dec_seq)
        xs = _merge_sample(attn_s, sas, g0s, parts, xs, w_br_b[l, 0], w_out_b[l], gf, final)
        ks_l.append(k_s); vs_l.append(v_s); cs_l.append(csts)

    y_prompt = xp.reshape(batch, seq, D_MODEL)
    y_sample = xs.reshape(dec_batch, dec_seq, D_MODEL)
    head_p = lambda ts: jnp.stack(ts).reshape(depth, batch, N_HEADS, HEAD_DIM, seq).transpose(0, 1, 4, 2, 3)
    head_s = lambda ts: jnp.stack(ts).reshape(depth, dec_batch, dec_seq, N_HEADS, HEAD_DIM)
    mem_shape = (depth, batch, MEM_LEN, MEM_HEADS, MEM_HEAD_DIM)
    return (y_prompt, y_sample, head_p(kp_l), head_p(vp_l), jnp.stack(cp_l),
            mem_k_p.reshape(mem_shape), mem_v_p.reshape(mem_shape),
            head_s(ks_l), head_s(vs_l), jnp.stack(cs_l))
```

```python
import functools

import jax
import jax.numpy as jnp
from jax import lax
from jax.experimental import pallas as pl
from jax.experimental.pallas import tpu as pltpu

F32 = jnp.float32
BF16 = jnp.bfloat16

D_MODEL = 1024
N_HEADS = 8
HEAD_DIM = 64
ATTN_W = N_HEADS * HEAD_DIM
ROT_DIM = HEAD_DIM // 4
ROPE_THETA = 500000.0
MOBA_BLOCK = 256
MOBA_TOP_K = 3
CONV_CH = 512
CONV_K = 31
CONV_HALO = CONV_K - 1
MEM_LEN = 256
MEM_HEADS = 4
MEM_HEAD_DIM = 128
MEM_W = MEM_HEADS * MEM_HEAD_DIM
PAGE_SIZE = 128
RMS_EPS = 1e-6
LN_EPS = 1e-5
ATTN_SCALE = HEAD_DIM ** -0.5
MEM_SCALE = MEM_HEAD_DIM ** -0.5

LANES = 128
SUBLANES = 8
LOG2E = 1.4426950408889634
ROW_TILE = 256
LOGITS_AHEAD = 3
PAGES_PER_STEP = 16
HALO_PAD = 32
VMEM_LIMIT = 56 * 1024 * 1024

_Q, _K, _V, _AG, _CV, _CG, _CGATE, _MQ, _MG, _MERGE = (
    0, 512, 1024, 1536, 2048, 2560, 3072, 3584, 4096, 4608)
N_IN = _MERGE + 3 * D_MODEL

_NT = (((1,), (1,)), ((), ()))


def _rms_norm(x, g):
    return x * lax.rsqrt(jnp.mean(x * x, axis=-1, keepdims=True) + RMS_EPS) * g


def _layer_norm(x, g, b):
    mu = jnp.mean(x, axis=-1, keepdims=True)
    xc = x - mu
    var = jnp.mean(xc * xc, axis=-1, keepdims=True)
    return xc * lax.rsqrt(var + LN_EPS) * g + b


def _rope(xc, cosf, sina, sinb):
    return (xc * cosf + pltpu.roll(xc, LANES - ROT_DIM // 2, 1) * sina
            + pltpu.roll(xc, ROT_DIM // 2, 1) * sinb)


def _rope_tables(pos):
    half = ROT_DIM // 2
    inv = ROPE_THETA ** (-jnp.arange(0, ROT_DIM, 2, dtype=F32) / ROT_DIM)
    ang = pos.astype(F32)[:, None] * inv[None, :]
    cos, sin = jnp.cos(ang), jnp.sin(ang)
    n = pos.shape[0]
    zeros_h = jnp.zeros((n, half), F32)
    rest0 = jnp.zeros((n, HEAD_DIM - ROT_DIM), F32)
    cosf = jnp.concatenate([cos, cos, jnp.ones((n, HEAD_DIM - ROT_DIM), F32)], axis=1)
    sina = jnp.concatenate([-sin, zeros_h, rest0], axis=1)
    sinb = jnp.concatenate([zeros_h, sin, rest0], axis=1)
    rep = LANES // HEAD_DIM
    return tuple(jnp.tile(t, (1, rep)) for t in (cosf, sina, sinb))


def _top_k_select(sc, valid, idx, n, axis=0):
    scm = jnp.where(valid, sc, -jnp.inf)
    rank = jnp.zeros(sc.shape, F32)
    for j in range(n):
        sj = scm[j:j + 1, :] if axis == 0 else scm[:, j:j + 1]
        beats = (sj > scm) | ((sj == scm) & (j < idx))
        rank = rank + jnp.where(beats, 1.0, 0.0)
    return valid & (rank < float(MOBA_TOP_K))


def _merge_out(attn, sa, g0, part, x, wb0, wout):
    a = (attn * sa).astype(BF16)
    pa = jnp.dot(a, wb0, preferred_element_type=F32)
    mix = (g0 * pa + part).astype(BF16)
    return x + jnp.dot(mix, wout, preferred_element_type=F32)


def _mem_proj_kernel(mem_ref, wk_ref, wv_ref, mk_out, mv_out):
    mb = mem_ref[...].astype(BF16)
    mk_out[...] = jnp.dot(mb, wk_ref[...].astype(BF16), preferred_element_type=F32)
    mv_out[...] = jnp.dot(mb, wv_ref[...].astype(BF16), preferred_element_type=F32)


def _mem_proj(mem2d, w_mem_k, w_mem_v):
    depth = w_mem_k.shape[0]
    rows = mem2d.shape[0]
    w_spec = pl.BlockSpec((None, D_MODEL, MEM_W), lambda l: (l, 0, 0))
    o_spec = pl.BlockSpec((None, rows, MEM_W), lambda l: (l, 0, 0))
    return pl.pallas_call(
        _mem_proj_kernel,
        grid=(depth,),
        in_specs=[pl.BlockSpec((rows, D_MODEL), lambda l: (0, 0)), w_spec, w_spec],
        out_specs=[o_spec, o_spec],
        out_shape=[jax.ShapeDtypeStruct((depth, rows, MEM_W), F32)] * 2,
        compiler_params=pltpu.CompilerParams(dimension_semantics=("arbitrary",),
                                             vmem_limit_bytes=VMEM_LIMIT),
        name="mem_proj",
    )(mem2d, w_mem_k, w_mem_v)


def _proj_prompt_kernel(x_ref, g_ref, w_ref, cos_ref, sina_ref, sinb_ref, cw_ref, cb_ref, lng_ref, lnb_ref,
                        mk_ref, mv_ref, wb1_ref, wb2_ref, *refs, tiles_per_seq, n_prev):
    if n_prev:
        ktp_ref, vtp_ref = refs[:2]
        refs = refs[2:]
    (q_out, kt_out, vt_out, kb_out, vtb_out, km_out, sa_out, g0_out, part_out, cst_out, ubuf, urot) = refs
    tm = ROW_TILE
    tin = pl.program_id(0) % tiles_per_seq
    if n_prev:
        kt_out[0:n_prev] = ktp_ref[...]
        vt_out[0:n_prev] = vtp_ref[...]
    hb = _rms_norm(x_ref[...], g_ref[...]).astype(BF16)

    def seg(a, width):
        return jnp.dot(hb, w_ref[:, a:a + width], preferred_element_type=F32)

    cosf, sina, sinb = cos_ref[...], sina_ref[...], sinb_ref[...]
    zq = seg(_Q, ATTN_W)
    zk = seg(_K, ATTN_W)
    for c in range(ATTN_W // LANES):
        sl = slice(c * LANES, (c + 1) * LANES)
        q_out[:, sl] = _rope(zq[:, sl], cosf, sina, sinb)
        kr = _rope(zk[:, sl], cosf, sina, sinb)
        kt_out[n_prev, sl, :] = kr.T
        kb_out[:, sl] = kr.astype(BF16)
        km_out[:, sl] = jnp.mean(kr, axis=0, keepdims=True)
    zvt = seg(_V, ATTN_W).T
    vt_out[n_prev] = zvt
    vtb_out[...] = zvt.astype(BF16)
    sa_out[...] = jax.nn.silu(seg(_AG, ATTN_W))

    u = seg(_CV, CONV_CH) * jax.nn.sigmoid(seg(_CG, CONV_CH))

    @pl.when(tin == 0)
    def _():
        ubuf[0:HALO_PAD, :] = jnp.zeros((HALO_PAD, CONV_CH), F32)

    @pl.when(tin != 0)
    def _():
        ubuf[HALO_PAD - CONV_HALO:HALO_PAD, :] = ubuf[tm + HALO_PAD - CONV_HALO:tm + HALO_PAD, :]

    ubuf[HALO_PAD:HALO_PAD + tm, :] = u
    nrot = tm + HALO_PAD - SUBLANES
    for r in range(1, SUBLANES):
        urot[r - 1] = ubuf[r:r + nrot, :]
    conv = jnp.broadcast_to(cb_ref[...], (tm, CONV_CH))
    for t in range(CONV_K):
        a, r = divmod(HALO_PAD - CONV_HALO + t, SUBLANES)
        rows = slice(a * SUBLANES, a * SUBLANES + tm)
        conv = conv + cw_ref[t:t + 1, :] * (ubuf[rows, :] if r == 0 else urot[r - 1, rows, :])
    cst_out[...] = ubuf[tm + HALO_PAD - CONV_HALO:tm + HALO_PAD, :]
    cbr = jax.nn.silu(_layer_norm(conv, lng_ref[...], lnb_ref[...])) * jax.nn.silu(seg(_CGATE, CONV_CH))

    zmq = seg(_MQ, MEM_W)
    mparts = []
    for hd in range(MEM_HEADS):
        sl = slice(hd * MEM_HEAD_DIM, (hd + 1) * MEM_HEAD_DIM)
        s = lax.dot_general(zmq[:, sl].astype(BF16), mk_ref[:, sl].astype(BF16), _NT,
                            preferred_element_type=F32) * MEM_SCALE
        p = jnp.exp(s - jnp.max(s, axis=-1, keepdims=True))
        o = jnp.dot(p.astype(BF16), mv_ref[:, sl].astype(BF16), preferred_element_type=F32)
        mparts.append(o / jnp.sum(p, axis=-1, keepdims=True))
    mbr = jnp.concatenate(mparts, axis=1) * jax.nn.silu(seg(_MG, MEM_W))

    pc = jnp.dot(cbr.astype(BF16), wb1_ref[...], preferred_element_type=F32)
    pm = jnp.dot(mbr.astype(BF16), wb2_ref[...], preferred_element_type=F32)
    g0_out[...] = jax.nn.sigmoid(seg(_MERGE, D_MODEL))
    part_out[...] = (jax.nn.sigmoid(seg(_MERGE + D_MODEL, D_MODEL)) * pc
                     + jax.nn.sigmoid(seg(_MERGE + 2 * D_MODEL, D_MODEL)) * pm)


def _const_spec(shape, ngrid=1):
    zeros = (0,) * len(shape)
    if ngrid == 1:
        return pl.BlockSpec(shape, lambda i: zeros)
    return pl.BlockSpec(shape, lambda i, j: zeros)


def _proj_prompt(x2d, g, w_in_b, tabs, cw, cb, lng, lnb, mk, mv, w_br_b, kv_prev, layer, batch, seq):
    tm = ROW_TILE
    n = batch * seq
    tps = seq // tm
    nt = n // tm
    n_prev = 0 if kv_prev is None else kv_prev[0].shape[0]
    row = lambda w: pl.BlockSpec((tm, w), lambda t: (t, 0))
    tab = pl.BlockSpec((tm, LANES), lambda t: (t % tps, 0))
    mem = pl.BlockSpec((None, MEM_LEN, MEM_W), lambda t: (layer, t // tps, 0))
    wbr = lambda br: pl.BlockSpec((None, None, ATTN_W, D_MODEL), lambda t: (layer, br, 0, 0))
    in_specs = [
        row(D_MODEL), _const_spec((1, D_MODEL)),
        pl.BlockSpec((None, D_MODEL, N_IN), lambda t: (layer, 0, 0), pipeline_mode=pl.Buffered(1)),
        tab, tab, tab,
        _const_spec((CONV_K, CONV_CH)), _const_spec((1, CONV_CH)), _const_spec((1, CONV_CH)),
        _const_spec((1, CONV_CH)),
        mem, mem, wbr(1), wbr(2),
    ]
    args = [x2d, g, w_in_b, *tabs, cw, cb, lng, lnb, mk, mv, w_br_b, w_br_b]
    if n_prev:
        prev = pl.BlockSpec((n_prev, None, ATTN_W, tm), lambda t: (0, t // tps, 0, t % tps))
        in_specs += [prev, prev]
        args += list(kv_prev)
    out_shape = [
        jax.ShapeDtypeStruct((n, ATTN_W), F32),
        jax.ShapeDtypeStruct((n_prev + 1, batch, ATTN_W, seq), F32),
        jax.ShapeDtypeStruct((n_prev + 1, batch, ATTN_W, seq), F32),
        jax.ShapeDtypeStruct((nt, tm, ATTN_W), BF16),
        jax.ShapeDtypeStruct((nt, ATTN_W, tm), BF16),
        jax.ShapeDtypeStruct((nt, 1, ATTN_W), F32),
        jax.ShapeDtypeStruct((n, ATTN_W), F32),
        jax.ShapeDtypeStruct((n, D_MODEL), F32),
        jax.ShapeDtypeStruct((n, D_MODEL), F32),
        jax.ShapeDtypeStruct((batch, CONV_HALO, CONV_CH), F32),
    ]
    seq_t = pl.BlockSpec((n_prev + 1, None, ATTN_W, tm), lambda t: (0, t // tps, 0, t % tps))
    out_specs = [
        row(ATTN_W), seq_t, seq_t,
        pl.BlockSpec((None, tm, ATTN_W), lambda t: (t, 0, 0)),
        pl.BlockSpec((None, ATTN_W, tm), lambda t: (t, 0, 0)),
        pl.BlockSpec((None, 1, ATTN_W), lambda t: (t, 0, 0)),
        row(ATTN_W), row(D_MODEL), row(D_MODEL),
        pl.BlockSpec((None, CONV_HALO, CONV_CH), lambda t: (t // tps, 0, 0)),
    ]
    return pl.pallas_call(
        functools.partial(_proj_prompt_kernel, tiles_per_seq=tps, n_prev=n_prev),
        grid=(nt,),
        in_specs=in_specs,
        out_specs=out_specs,
        out_shape=out_shape,
        scratch_shapes=[pltpu.VMEM((tm + HALO_PAD, CONV_CH), F32),
                        pltpu.VMEM((SUBLANES - 1, tm + HALO_PAD - SUBLANES, CONV_CH), F32)],
        compiler_params=pltpu.CompilerParams(dimension_semantics=("arbitrary",),
                                             vmem_limit_bytes=VMEM_LIMIT),
        name="proj_prompt",
    )(*args)


def _attn_prompt_kernel(q_ref, kb_ref, vt_ref, km_ref, sa_ref, g0_ref, part_ref, x_ref, wb0_ref, wout_ref,
                        gf_ref, o_ref, sel_scr, qb_scr, m_scr, l_scr, acc_scr, *, final, nblk):
    tq = ROW_TILE
    i = pl.program_id(1)
    blk_idx = lax.broadcasted_iota(jnp.int32, (nblk, tq), 0)
    valid = blk_idx < i
    lane = lax.broadcasted_iota(jnp.int32, (tq, LANES), 1)
    causal = (lax.broadcasted_iota(jnp.int32, (MOBA_BLOCK, tq), 0)
              <= lax.broadcasted_iota(jnp.int32, (MOBA_BLOCK, tq), 1))
    pair = lambda h: slice((h // 2) * LANES, (h // 2 + 1) * LANES)
    rows = lambda h: slice(h * HEAD_DIM, (h + 1) * HEAD_DIM)
    bcast = lambda r: jnp.broadcast_to(r, (SUBLANES, tq))

    for h in range(N_HEADS):
        qm = jnp.where((lane // HEAD_DIM) == h % 2, q_ref[:, pair(h)], 0.0)
        sc = lax.dot_general(km_ref[:, pair(h)], qm, _NT, precision=lax.Precision.HIGHEST,
                             preferred_element_type=F32)
        sel_scr[h] = jnp.where(_top_k_select(sc, valid, blk_idx, nblk), 1.0, 0.0)
        qmb = (qm * (ATTN_SCALE * LOG2E)).astype(BF16)
        qb_scr[h] = qmb
        s = lax.dot_general(kb_ref[i, :, pair(h)], qmb, _NT, preferred_element_type=F32)
        s = jnp.where(causal, s, -jnp.inf)
        m0 = jnp.max(s, axis=0, keepdims=True)
        p = jnp.exp2(s - m0)
        m_scr[h] = bcast(m0)
        l_scr[h] = bcast(jnp.sum(p, axis=0, keepdims=True))
        acc_scr[rows(h), :] = jnp.dot(vt_ref[i, rows(h), :], p.astype(BF16), preferred_element_type=F32)

    def logits(j, h):
        return lax.dot_general(kb_ref[j, :, pair(h)], qb_scr[h], _NT, preferred_element_type=F32)

    def body(j, ahead):
        ahead = list(ahead)
        for h in range(N_HEADS):
            s = ahead.pop(0)
            nh = h + LOGITS_AHEAD
            ahead.append(logits(j, nh) if nh < N_HEADS else logits(j + 1, nh - N_HEADS))
            s = jnp.where(sel_scr[h, pl.ds(j, 1), :] > 0.0, s, -jnp.inf)
            m_old = m_scr[h, 0:1, :]
            m_new = jnp.maximum(m_old, jnp.max(s, axis=0, keepdims=True))
            alpha = jnp.exp2(m_old - m_new)
            p = jnp.exp2(s - m_new)
            m_scr[h] = bcast(m_new)
            l_scr[h] = bcast(alpha * l_scr[h, 0:1, :] + jnp.sum(p, axis=0, keepdims=True))
            acc_scr[rows(h), :] = alpha * acc_scr[rows(h), :] + jnp.dot(
                vt_ref[j, rows(h), :], p.astype(BF16), preferred_element_type=F32)
        return tuple(ahead)

    lax.fori_loop(0, i, body, tuple(logits(0, h) for h in range(LOGITS_AHEAD)))
    for h in range(N_HEADS):
        acc_scr[rows(h), :] = acc_scr[rows(h), :] / l_scr[h, 0:1, :]

    attn = acc_scr[...].T
    xn = _merge_out(attn, sa_ref[...], g0_ref[...], part_ref[...], x_ref[...], wb0_ref[...], wout_ref[...])
    if final:
        xn = _rms_norm(xn, gf_ref[...])
    o_ref[...] = xn


def _attn_prompt(q, kb, vt, km, sa, g0, part, x2d, w_br_b, w_out_b, gf, layer, batch, seq, final):
    tq = ROW_TILE
    nblk = seq // MOBA_BLOCK
    row = lambda w: pl.BlockSpec((tq, w), lambda b, i: (b * nblk + i, 0))
    in_specs = [
        row(ATTN_W),
        pl.BlockSpec((nblk, MOBA_BLOCK, ATTN_W), lambda b, i: (b, 0, 0)),
        pl.BlockSpec((nblk, ATTN_W, MOBA_BLOCK), lambda b, i: (b, 0, 0)),
        pl.BlockSpec((None, nblk, ATTN_W), lambda b, i: (b, 0, 0)),
        row(ATTN_W), row(D_MODEL), row(D_MODEL), row(D_MODEL),
        pl.BlockSpec((None, None, ATTN_W, D_MODEL), lambda b, i: (layer, 0, 0, 0)),
        pl.BlockSpec((None, D_MODEL, D_MODEL), lambda b, i: (layer, 0, 0)),
        _const_spec((1, D_MODEL), 2),
    ]
    return pl.pallas_call(
        functools.partial(_attn_prompt_kernel, final=final, nblk=nblk),
        grid=(batch, nblk),
        in_specs=in_specs,
        out_specs=row(D_MODEL),
        out_shape=jax.ShapeDtypeStruct((batch * seq, D_MODEL), F32),
        scratch_shapes=[pltpu.VMEM((N_HEADS, nblk, tq), F32),
                        pltpu.VMEM((N_HEADS, tq, LANES), BF16),
                        pltpu.VMEM((N_HEADS, SUBLANES, tq), F32),
                        pltpu.VMEM((N_HEADS, SUBLANES, tq), F32),
                        pltpu.VMEM((ATTN_W, tq), F32)],
        compiler_params=pltpu.CompilerParams(dimension_semantics=("arbitrary", "arbitrary"),
                                             vmem_limit_bytes=VMEM_LIMIT),
        name="attn_prompt",
    )(q, kb, vt, km, sa, g0, part, x2d, w_br_b, w_out_b, gf)


def _proj_sample_kernel(x_ref, g_ref, w_ref, cos_ref, sina_ref, sinb_ref, cw_ref, cb_ref, lng_ref, lnb_ref,
                        st_ref, mk_ref, mv_ref, wb1_ref, wb2_ref,
                        q_out, k_out, v_out, sa_out, g0_out, part_out, cst_out,
                        u_scr, cgate_scr, mq_scr, mgate_scr, g12_scr, c_scr, m_scr, full_scr, *, dec_seq):
    b = pl.program_id(0)
    nb = pl.num_programs(0)

    @pl.when(b == 0)
    def _():
        hb = _rms_norm(x_ref[...], g_ref[...]).astype(BF16)

        def seg(a, width):
            return jnp.dot(hb, w_ref[:, a:a + width], preferred_element_type=F32)

        cosf, sina, sinb = cos_ref[...], sina_ref[...], sinb_ref[...]
        zq = seg(_Q, ATTN_W)
        zk = seg(_K, ATTN_W)
        for c in range(ATTN_W // LANES):
            sl = slice(c * LANES, (c + 1) * LANES)
            q_out[:, sl] = _rope(zq[:, sl], cosf, sina, sinb)
            k_out[:, sl] = _rope(zk[:, sl], cosf, sina, sinb)
        v_out[...] = seg(_V, ATTN_W)
        sa_out[...] = jax.nn.silu(seg(_AG, ATTN_W))
        u_scr[...] = seg(_CV, CONV_CH) * jax.nn.sigmoid(seg(_CG, CONV_CH))
        cgate_scr[...] = jax.nn.silu(seg(_CGATE, CONV_CH))
        mq_scr[...] = seg(_MQ, MEM_W)
        mgate_scr[...] = jax.nn.silu(seg(_MG, MEM_W))
        g0_out[...] = jax.nn.sigmoid(seg(_MERGE, D_MODEL))
        g12_scr[:, 0:D_MODEL] = jax.nn.sigmoid(seg(_MERGE + D_MODEL, D_MODEL))
        g12_scr[:, D_MODEL:2 * D_MODEL] = jax.nn.sigmoid(seg(_MERGE + 2 * D_MODEL, D_MODEL))

    r0 = pl.multiple_of(b * dec_seq, dec_seq)
    rows = pl.ds(r0, dec_seq)

    full_scr[0:CONV_HALO, :] = st_ref[...]
    full_scr[CONV_HALO:CONV_HALO + dec_seq, :] = u_scr[rows, :]
    conv = jnp.broadcast_to(cb_ref[...], (dec_seq, CONV_CH))
    for t in range(CONV_K):
        conv = conv + cw_ref[t:t + 1, :] * full_scr[t:t + dec_seq, :]
    cst_out[...] = full_scr[dec_seq:dec_seq + CONV_HALO, :]
    c_scr[rows, :] = jax.nn.silu(_layer_norm(conv, lng_ref[...], lnb_ref[...])) * cgate_scr[rows, :]

    mq = mq_scr[rows, :]
    mparts = []
    for hd in range(MEM_HEADS):
        sl = slice(hd * MEM_HEAD_DIM, (hd + 1) * MEM_HEAD_DIM)
        s = lax.dot_general(mq[:, sl], mk_ref[:, hd, :], _NT, preferred_element_type=F32) * MEM_SCALE
        p = jnp.exp(s - jnp.max(s, axis=-1, keepdims=True))
        o = jnp.dot(p, mv_ref[:, hd, :], preferred_element_type=F32)
        mparts.append(o / jnp.sum(p, axis=-1, keepdims=True))
    m_scr[rows, :] = jnp.concatenate(mparts, axis=1) * mgate_scr[rows, :]

    @pl.when(b == nb - 1)
    def _():
        pc = jnp.dot(c_scr[...].astype(BF16), wb1_ref[...], preferred_element_type=F32)
        pm = jnp.dot(m_scr[...].astype(BF16), wb2_ref[...], preferred_element_type=F32)
        part_out[...] = g12_scr[:, 0:D_MODEL] * pc + g12_scr[:, D_MODEL:2 * D_MODEL] * pm


def _proj_sample(x2d, g, w_in_b, tabs, cw, cb, lng, lnb, state_conv, cache_mem_k, cache_mem_v, w_br_b,
                 layer, dec_batch, dec_seq):
    n = dec_batch * dec_seq
    full = lambda w: _const_spec((n, w))
    mem = pl.BlockSpec((None, None, MEM_LEN, MEM_HEADS, MEM_HEAD_DIM), lambda b: (layer, b, 0, 0, 0))
    wbr = lambda br: pl.BlockSpec((None, None, ATTN_W, D_MODEL), lambda b: (layer, br, 0, 0))
    in_specs = [
        full(D_MODEL), _const_spec((1, D_MODEL)),
        pl.BlockSpec((None, D_MODEL, N_IN), lambda b: (layer, 0, 0), pipeline_mode=pl.Buffered(1)),
        full(LANES), full(LANES), full(LANES),
        _const_spec((CONV_K, CONV_CH)), _const_spec((1, CONV_CH)), _const_spec((1, CONV_CH)),
        _const_spec((1, CONV_CH)),
        pl.BlockSpec((None, None, CONV_HALO, CONV_CH), lambda b: (layer, b, 0, 0)),
        mem, mem, wbr(1), wbr(2),
    ]
    out_shape = [jax.ShapeDtypeStruct((n, ATTN_W), F32)] * 4 + [
        jax.ShapeDtypeStruct((n, D_MODEL), F32), jax.ShapeDtypeStruct((n, D_MODEL), F32),
        jax.ShapeDtypeStruct((dec_batch, CONV_HALO, CONV_CH), F32)]
    out_specs = [full(ATTN_W)] * 4 + [full(D_MODEL), full(D_MODEL),
                                      pl.BlockSpec((None, CONV_HALO, CONV_CH), lambda b: (b, 0, 0))]
    scr = lambda w: pltpu.VMEM((n, w), F32)
    return pl.pallas_call(
        functools.partial(_proj_sample_kernel, dec_seq=dec_seq),
        grid=(dec_batch,),
        in_specs=in_specs,
        out_specs=out_specs,
        out_shape=out_shape,
        scratch_shapes=[scr(CONV_CH), scr(CONV_CH), scr(MEM_W), scr(MEM_W), scr(2 * D_MODEL),
                        scr(CONV_CH), scr(MEM_W),
                        pltpu.VMEM((CONV_HALO + dec_seq + 2, CONV_CH), F32)],
        compiler_params=pltpu.CompilerParams(dimension_semantics=("arbitrary",),
                                             vmem_limit_bytes=VMEM_LIMIT),
        name="proj_sample",
    )(x2d, g, w_in_b, *tabs, cw, cb, lng, lnb, state_conv, cache_mem_k, cache_mem_v, w_br_b, w_br_b)


def _page_index(layer, r, b, c, pt_ref):
    return (layer, pt_ref[b, c * PAGES_PER_STEP + r], 0, 0, 0)


def _head_rows(x, dec_seq):
    rows = N_HEADS * dec_seq
    tiled = jnp.concatenate([x] * N_HEADS, axis=0)
    row_h = lax.broadcasted_iota(jnp.int32, (rows, ATTN_W), 0) // dec_seq
    lane_h = lax.broadcasted_iota(jnp.int32, (rows, ATTN_W), 1) // HEAD_DIM
    return jnp.where(row_h == lane_h, tiled, 0.0)


def _sample_logits_kernel(pt_ref, q_ref, kn_ref, *refs, dec_seq, n_pages):
    kp_refs = refs[:PAGES_PER_STEP]
    p_out, l_out = refs[PAGES_PER_STEP:PAGES_PER_STEP + 2]
    logit_scr, qb_scr = refs[PAGES_PER_STEP + 2:]
    c = pl.program_id(1)
    nc = pl.num_programs(1)
    rows = N_HEADS * dec_seq
    pages_per_blk = MOBA_BLOCK // PAGE_SIZE
    nblk = n_pages // pages_per_blk

    @pl.when(c == 0)
    def _():
        qb_scr[...] = (_head_rows(q_ref[...], dec_seq) * ATTN_SCALE).astype(BF16)

    qb = qb_scr[...]
    for r in range(PAGES_PER_STEP):
        kt = kp_refs[r][...].reshape(ATTN_W, PAGE_SIZE)
        logit_scr[c * PAGES_PER_STEP + r] = jnp.dot(qb, kt.astype(BF16), preferred_element_type=F32)

    @pl.when(c == nc - 1)
    def _():
        blk_idx = lax.broadcasted_iota(jnp.int32, (rows, nblk), 1)
        sc = jnp.zeros((rows, nblk), F32)
        for j in range(nblk):
            blk = logit_scr[j * pages_per_blk]
            for pp in range(1, pages_per_blk):
                blk = blk + logit_scr[j * pages_per_blk + pp]
            sc = jnp.where(blk_idx == j, jnp.sum(blk, axis=-1, keepdims=True), sc)
        sel = _top_k_select(sc, blk_idx >= 0, blk_idx, nblk, axis=1)
        sel_t = jnp.where(sel, 1.0, 0.0)

        kn_page = jnp.concatenate([kn_ref[...], jnp.zeros((PAGE_SIZE - dec_seq, ATTN_W), F32)], axis=0)
        ln = lax.dot_general(qb, kn_page.astype(BF16), _NT, preferred_element_type=F32)
        key_i = lax.broadcasted_iota(jnp.int32, (rows, PAGE_SIZE), 1)
        qry_i = lax.broadcasted_iota(jnp.int32, (rows, PAGE_SIZE), 0) % dec_seq
        ln = jnp.where(key_i <= qry_i, ln, -jnp.inf)

        mx = ln
        for p in range(n_pages):
            j = p // pages_per_blk
            lp = jnp.where(sel_t[:, j:j + 1] > 0.0, logit_scr[p], -jnp.inf)
            logit_scr[p] = lp
            mx = jnp.maximum(mx, lp)
        m = jnp.max(mx, axis=-1, keepdims=True)
        pn = jnp.exp(ln - m)
        p_out[n_pages] = pn.astype(BF16)
        lsum = pn
        for p in range(n_pages):
            pp = jnp.exp(logit_scr[p] - m)
            p_out[p] = pp.astype(BF16)
            lsum = lsum + pp
        l_out[...] = jnp.broadcast_to(jnp.sum(lsum, axis=-1, keepdims=True), (rows, LANES))


def _sample_values_kernel(pt_ref, p_ref, pn_ref, l_ref, vn_ref, *refs, dec_seq):
    vp_refs = refs[:PAGES_PER_STEP]
    o_ref = refs[PAGES_PER_STEP]
    acc_scr = refs[PAGES_PER_STEP + 1]
    c = pl.program_id(1)
    nc = pl.num_programs(1)

    @pl.when(c == 0)
    def _():
        vn_page = jnp.concatenate([vn_ref[...], jnp.zeros((PAGE_SIZE - dec_seq, ATTN_W), F32)], axis=0)
        acc_scr[...] = jnp.dot(pn_ref[...], vn_page.astype(BF16), preferred_element_type=F32)

    acc = acc_scr[...]
    for r in range(PAGES_PER_STEP):
        vt = vp_refs[r][...].reshape(ATTN_W, PAGE_SIZE)
        acc = acc + lax.dot_general(p_ref[r], vt.astype(BF16), _NT, preferred_element_type=F32)
    acc_scr[...] = acc

    @pl.when(c == nc - 1)
    def _():
        res = acc / jnp.concatenate([l_ref[...]] * (ATTN_W // LANES), axis=1)
        lane_h = lax.broadcasted_iota(jnp.int32, (dec_seq, ATTN_W), 1) // HEAD_DIM
        out = jnp.zeros((dec_seq, ATTN_W), F32)
        for h in range(N_HEADS):
            out = out + jnp.where(lane_h == h, res[h * dec_seq:(h + 1) * dec_seq, :], 0.0)
        o_ref[...] = out


def _attn_sample(page_table, q, k_new, v_new, cache_k, cache_v, layer, dec_batch, dec_seq):
    n_pages = page_table.shape[1]
    nc = n_pages // PAGES_PER_STEP
    rows = N_HEADS * dec_seq
    page_specs = [pl.BlockSpec((None, None, N_HEADS, HEAD_DIM, PAGE_SIZE),
                               functools.partial(_page_index, layer, r))
                  for r in range(PAGES_PER_STEP)]
    seq_rows = pl.BlockSpec((dec_seq, ATTN_W), lambda b, c, pt: (b, 0))
    cparams = pltpu.CompilerParams(dimension_semantics=("arbitrary", "arbitrary"),
                                   vmem_limit_bytes=VMEM_LIMIT)

    p_all, l_all = pl.pallas_call(
        functools.partial(_sample_logits_kernel, dec_seq=dec_seq, n_pages=n_pages),
        grid_spec=pltpu.PrefetchScalarGridSpec(
            num_scalar_prefetch=1,
            grid=(dec_batch, nc),
            in_specs=[seq_rows, seq_rows] + page_specs,
            out_specs=[pl.BlockSpec((None, n_pages + 1, rows, PAGE_SIZE), lambda b, c, pt: (b, 0, 0, 0)),
                       pl.BlockSpec((None, rows, LANES), lambda b, c, pt: (b, 0, 0))],
            scratch_shapes=[pltpu.VMEM((n_pages, rows, PAGE_SIZE), F32),
                            pltpu.VMEM((rows, ATTN_W), BF16)]),
        out_shape=[jax.ShapeDtypeStruct((dec_batch, n_pages + 1, rows, PAGE_SIZE), BF16),
                   jax.ShapeDtypeStruct((dec_batch, rows, LANES), F32)],
        compiler_params=cparams,
        name="sample_logits",
    )(page_table, q, k_new, *([cache_k] * PAGES_PER_STEP))

    return pl.pallas_call(
        functools.partial(_sample_values_kernel, dec_seq=dec_seq),
        grid_spec=pltpu.PrefetchScalarGridSpec(
            num_scalar_prefetch=1,
            grid=(dec_batch, nc),
            in_specs=[pl.BlockSpec((None, PAGES_PER_STEP, rows, PAGE_SIZE), lambda b, c, pt: (b, c, 0, 0)),
                      pl.BlockSpec((None, None, rows, PAGE_SIZE), lambda b, c, pt: (b, n_pages, 0, 0)),
                      pl.BlockSpec((None, rows, LANES), lambda b, c, pt: (b, 0, 0)),
                      seq_rows] + page_specs,
            out_specs=seq_rows,
            scratch_shapes=[pltpu.VMEM((rows, ATTN_W), F32)]),
        out_shape=jax.ShapeDtypeStruct((dec_batch * dec_seq, ATTN_W), F32),
        compiler_params=cparams,
        name="sample_values",
    )(page_table, p_all, p_all, l_all, v_new, *([cache_v] * PAGES_PER_STEP))


def _merge_sample_kernel(a_ref, sa_ref, g0_ref, part_ref, x_ref, wb0_ref, wout_ref, gf_ref, o_ref, *, final):
    xn = _merge_out(a_ref[...], sa_ref[...], g0_ref[...], part_ref[...], x_ref[...], wb0_ref[...],
                    wout_ref[...])
    if final:
        xn = _rms_norm(xn, gf_ref[...])
    o_ref[...] = xn


def _merge_sample(attn, sa, g0, part, x2d, w_br_b, w_out_b, gf, layer, final):
    n = x2d.shape[0]
    full = lambda w: _const_spec((n, w))
    return pl.pallas_call(
        functools.partial(_merge_sample_kernel, final=final),
        grid=(1,),
        in_specs=[full(ATTN_W), full(ATTN_W), full(D_MODEL), full(D_MODEL), full(D_MODEL),
                  pl.BlockSpec((None, None, ATTN_W, D_MODEL), lambda i: (layer, 0, 0, 0)),
                  pl.BlockSpec((None, D_MODEL, D_MODEL), lambda i: (layer, 0, 0)), _const_spec((1, D_MODEL))],
        out_specs=full(D_MODEL),
        out_shape=jax.ShapeDtypeStruct((n, D_MODEL), F32),
        compiler_params=pltpu.CompilerParams(dimension_semantics=("arbitrary",),
                                             vmem_limit_bytes=VMEM_LIMIT),
        name="merge_sample",
    )(attn, sa, g0, part, x2d, w_br_b, w_out_b, gf)


def kernel(x_prompt, x_sample, cache_k, cache_v, cache_mem_k, cache_mem_v, state_conv, page_table, mem_prompt,
           g_norm, w_in, conv_w, conv_b, ln_g, ln_b, w_mem_k, w_mem_v, w_branch, w_out, g_final):
    batch, seq, _ = x_prompt.shape
    dec_batch, dec_seq, _ = x_sample.shape
    depth = w_in.shape[0]
    past_len = page_table.shape[1] * PAGE_SIZE
    assert seq % ROW_TILE == 0 and ROW_TILE == MOBA_BLOCK
    assert dec_batch * dec_seq == ROW_TILE and dec_seq == 8 and past_len % MOBA_BLOCK == 0
    assert page_table.shape[1] % PAGES_PER_STEP == 0

    w_in_b = w_in.astype(BF16)
    w_br_b = w_branch.astype(BF16)
    w_out_b = w_out.astype(BF16)
    gf = g_final.reshape(1, D_MODEL)

    mem_k_p, mem_v_p = _mem_proj(mem_prompt.reshape(batch * MEM_LEN, D_MODEL), w_mem_k, w_mem_v)

    tabs_p = _rope_tables(jnp.arange(seq, dtype=jnp.int32))
    tabs_s = tuple(jnp.tile(t, (dec_batch, 1))
                   for t in _rope_tables(past_len + jnp.arange(dec_seq, dtype=jnp.int32)))

    cache_kt = cache_k.transpose(0, 1, 3, 4, 2)
    cache_vt = cache_v.transpose(0, 1, 3, 4, 2)

    xp = x_prompt.reshape(batch * seq, D_MODEL)
    xs = x_sample.reshape(dec_batch * dec_seq, D_MODEL)
    kv_p = None
    cp_l, ks_l, vs_l, cs_l = [], [], [], []
    for l in range(depth):
        g = g_norm[l].reshape(1, D_MODEL)
        cw, cb = conv_w[l], conv_b[l].reshape(1, CONV_CH)
        lng, lnb = ln_g[l].reshape(1, CONV_CH), ln_b[l].reshape(1, CONV_CH)
        final = l == depth - 1

        q, kt, vt, kb, vtb, km, sa, g0, part, cst = _proj_prompt(
            xp, g, w_in_b, tabs_p, cw, cb, lng, lnb, mem_k_p, mem_v_p, w_br_b, kv_p, l, batch, seq)
        xp = _attn_prompt(q, kb, vtb, km.reshape(batch, seq // MOBA_BLOCK, ATTN_W), sa, g0, part, xp,
                          w_br_b, w_out_b, gf, l, batch, seq, final)
        kv_p = (kt, vt)
        cp_l.append(cst)

        qs, k_s, v_s, sas, g0s, parts, csts = _proj_sample(
            xs, g, w_in_b, tabs_s, cw, cb, lng, lnb, state_conv, cache_mem_k, cache_mem_v, w_br_b,
            l, dec_batch, dec_seq)
        attn_s = _attn_sample(page_table, qs, k_s, v_s, cache_kt, cache_vt, l, dec_batch, dec_seq)
        xs = _merge_sample(attn_s, sas, g0s, parts, xs, w_br_b, w_out_b, gf, l, final)
        ks_l.append(k_s); vs_l.append(v_s); cs_l.append(csts)

    y_prompt = xp.reshape(batch, seq, D_MODEL)
    y_sample = xs.reshape(dec_batch, dec_seq, D_MODEL)
    head_p = lambda t: t.reshape(depth, batch, N_HEADS, HEAD_DIM, seq).transpose(0, 1, 4, 2, 3)
    head_s = lambda ts: jnp.stack(ts).reshape(depth, dec_batch, dec_seq, N_HEADS, HEAD_DIM)
    mem_shape = (depth, batch, MEM_LEN, MEM_HEADS, MEM_HEAD_DIM)
    return (y_prompt, y_sample, head_p(kv_p[0]), head_p(kv_p[1]), jnp.stack(cp_l),
            mem_k_p.reshape(mem_shape), mem_v_p.reshape(mem_shape),
            head_s(ks_l), head_s(vs_l), jnp.stack(cs_l))
```

```python
import functools

import jax
import jax.numpy as jnp
from jax import lax
from jax.experimental import pallas as pl
from jax.experimental.pallas import tpu as pltpu

F32 = jnp.float32
BF16 = jnp.bfloat16

D_MODEL = 1024
N_HEADS = 8
HEAD_DIM = 64
ATTN_W = N_HEADS * HEAD_DIM
ROT_DIM = HEAD_DIM // 4
ROPE_THETA = 500000.0
MOBA_BLOCK = 256
MOBA_TOP_K = 3
CONV_CH = 512
CONV_K = 31
CONV_HALO = CONV_K - 1
MEM_LEN = 256
MEM_HEADS = 4
MEM_HEAD_DIM = 128
MEM_W = MEM_HEADS * MEM_HEAD_DIM
PAGE_SIZE = 128
RMS_EPS = 1e-6
LN_EPS = 1e-5
ATTN_SCALE = HEAD_DIM ** -0.5
MEM_SCALE = MEM_HEAD_DIM ** -0.5

LANES = 128
SUBLANES = 8
LOG2E = 1.4426950408889634
ROW_TILE = 256
LOGITS_AHEAD = 3
FUSED_PAGES = 32
HALO_PAD = 32
VMEM_LIMIT = 56 * 1024 * 1024

_Q, _K, _V, _AG, _CV, _CG, _CGATE, _MQ, _MG, _MERGE = (
    0, 512, 1024, 1536, 2048, 2560, 3072, 3584, 4096, 4608)
N_IN = _MERGE + 3 * D_MODEL

_NT = (((1,), (1,)), ((), ()))


def _rms_norm(x, g):
    return x * lax.rsqrt(jnp.mean(x * x, axis=-1, keepdims=True) + RMS_EPS) * g


def _layer_norm(x, g, b):
    mu = jnp.mean(x, axis=-1, keepdims=True)
    xc = x - mu
    var = jnp.mean(xc * xc, axis=-1, keepdims=True)
    return xc * lax.rsqrt(var + LN_EPS) * g + b


def _rope(xc, cosf, sina, sinb):
    return (xc * cosf + pltpu.roll(xc, LANES - ROT_DIM // 2, 1) * sina
            + pltpu.roll(xc, ROT_DIM // 2, 1) * sinb)


def _rope_tables(pos):
    half = ROT_DIM // 2
    inv = ROPE_THETA ** (-jnp.arange(0, ROT_DIM, 2, dtype=F32) / ROT_DIM)
    ang = pos.astype(F32)[:, None] * inv[None, :]
    cos, sin = jnp.cos(ang), jnp.sin(ang)
    n = pos.shape[0]
    zeros_h = jnp.zeros((n, half), F32)
    rest0 = jnp.zeros((n, HEAD_DIM - ROT_DIM), F32)
    cosf = jnp.concatenate([cos, cos, jnp.ones((n, HEAD_DIM - ROT_DIM), F32)], axis=1)
    sina = jnp.concatenate([-sin, zeros_h, rest0], axis=1)
    sinb = jnp.concatenate([zeros_h, sin, rest0], axis=1)
    rep = LANES // HEAD_DIM
    return tuple(jnp.tile(t, (1, rep)) for t in (cosf, sina, sinb))


def _top_k_select(sc, valid, idx, n, axis=0):
    scm = jnp.where(valid, sc, -jnp.inf)
    rank = jnp.zeros(sc.shape, F32)
    for j in range(n):
        sj = scm[j:j + 1, :] if axis == 0 else scm[:, j:j + 1]
        beats = (sj > scm) | ((sj == scm) & (j < idx))
        rank = rank + jnp.where(beats, 1.0, 0.0)
    return valid & (rank < float(MOBA_TOP_K))


def _merge_out(attn, sa, g0, part, x, wb0, wout):
    a = (attn * sa).astype(BF16)
    pa = jnp.dot(a, wb0, preferred_element_type=F32)
    mix = (g0 * pa + part).astype(BF16)
    return x + jnp.dot(mix, wout, preferred_element_type=F32)


def _mem_proj_kernel(mem_ref, wk_ref, wv_ref, mk_out, mv_out):
    mb = mem_ref[...].astype(BF16)
    mk_out[...] = jnp.dot(mb, wk_ref[...].astype(BF16), preferred_element_type=F32)
    mv_out[...] = jnp.dot(mb, wv_ref[...].astype(BF16), preferred_element_type=F32)


def _mem_proj(mem2d, w_mem_k, w_mem_v):
    depth = w_mem_k.shape[0]
    rows = mem2d.shape[0]
    w_spec = pl.BlockSpec((None, D_MODEL, MEM_W), lambda l: (l, 0, 0))
    o_spec = pl.BlockSpec((None, rows, MEM_W), lambda l: (l, 0, 0))
    return pl.pallas_call(
        _mem_proj_kernel,
        grid=(depth,),
        in_specs=[pl.BlockSpec((rows, D_MODEL), lambda l: (0, 0)), w_spec, w_spec],
        out_specs=[o_spec, o_spec],
        out_shape=[jax.ShapeDtypeStruct((depth, rows, MEM_W), F32)] * 2,
        compiler_params=pltpu.CompilerParams(dimension_semantics=("arbitrary",),
                                             vmem_limit_bytes=VMEM_LIMIT),
        name="mem_proj",
    )(mem2d, w_mem_k, w_mem_v)


def _proj_prompt_kernel(pt_ref, x_ref, g_ref, w_ref, cos_ref, sina_ref, sinb_ref, cw_ref, cb_ref, lng_ref, lnb_ref,
                        mk_ref, mv_ref, wb1_ref, wb2_ref, qs_ref, kn_ref, cache_ref, *refs,
                        tiles_per_seq, n_prev, layer, dec_seq, n_pages):
    if n_prev:
        ktp_ref, vtp_ref = refs[:2]
        refs = refs[2:]
    (q_out, kt_out, vt_out, kb_out, vtb_out, km_out, sa_out, g0_out, part_out, cst_out, p_out, l_out,
     ubuf, urot, page_buf, page_sem, logit_scr, qb_scr) = refs
    tm = ROW_TILE
    step = pl.program_id(0)
    tin = step % tiles_per_seq
    steps_per_seq = n_pages // FUSED_PAGES
    sb, chunk = step // steps_per_seq, step % steps_per_seq
    _page_copies(cache_ref, pt_ref, page_buf, page_sem, layer, sb, chunk, start=True)
    if n_prev:
        kt_out[0:n_prev] = ktp_ref[...]
        vt_out[0:n_prev] = vtp_ref[...]
    hb = _rms_norm(x_ref[...], g_ref[...]).astype(BF16)

    def seg(a, width):
        return jnp.dot(hb, w_ref[:, a:a + width], preferred_element_type=F32)

    cosf, sina, sinb = cos_ref[...], sina_ref[...], sinb_ref[...]
    zq = seg(_Q, ATTN_W)
    zk = seg(_K, ATTN_W)
    for c in range(ATTN_W // LANES):
        sl = slice(c * LANES, (c + 1) * LANES)
        q_out[:, sl] = _rope(zq[:, sl], cosf, sina, sinb)
        kr = _rope(zk[:, sl], cosf, sina, sinb)
        kt_out[n_prev, sl, :] = kr.T
        kb_out[:, sl] = kr.astype(BF16)
        km_out[:, sl] = jnp.mean(kr, axis=0, keepdims=True)
    zvt = seg(_V, ATTN_W).T
    vt_out[n_prev] = zvt
    vtb_out[...] = zvt.astype(BF16)
    sa_out[...] = jax.nn.silu(seg(_AG, ATTN_W))

    u = seg(_CV, CONV_CH) * jax.nn.sigmoid(seg(_CG, CONV_CH))

    @pl.when(tin == 0)
    def _():
        ubuf[0:HALO_PAD, :] = jnp.zeros((HALO_PAD, CONV_CH), F32)

    @pl.when(tin != 0)
    def _():
        ubuf[HALO_PAD - CONV_HALO:HALO_PAD, :] = ubuf[tm + HALO_PAD - CONV_HALO:tm + HALO_PAD, :]

    ubuf[HALO_PAD:HALO_PAD + tm, :] = u
    nrot = tm + HALO_PAD - SUBLANES
    for r in range(1, SUBLANES):
        urot[r - 1] = ubuf[r:r + nrot, :]
    cst_out[...] = ubuf[tm + HALO_PAD - CONV_HALO:tm + HALO_PAD, :]
    conv = jnp.broadcast_to(cb_ref[...], (tm, CONV_CH))
    for t in range(CONV_K):
        a, r = divmod(HALO_PAD - CONV_HALO + t, SUBLANES)
        rows = slice(a * SUBLANES, a * SUBLANES + tm)
        conv = conv + cw_ref[t:t + 1, :] * (ubuf[rows, :] if r == 0 else urot[r - 1, rows, :])
    cbr = jax.nn.silu(_layer_norm(conv, lng_ref[...], lnb_ref[...])) * jax.nn.silu(seg(_CGATE, CONV_CH))

    zmq = seg(_MQ, MEM_W)
    mparts = []
    for hd in range(MEM_HEADS):
        sl = slice(hd * MEM_HEAD_DIM, (hd + 1) * MEM_HEAD_DIM)
        s = lax.dot_general(zmq[:, sl].astype(BF16), mk_ref[:, sl].astype(BF16), _NT,
                            preferred_element_type=F32) * MEM_SCALE
        p = jnp.exp(s - jnp.max(s, axis=-1, keepdims=True))
        o = jnp.dot(p.astype(BF16), mv_ref[:, sl].astype(BF16), preferred_element_type=F32)
        mparts.append(o / jnp.sum(p, axis=-1, keepdims=True))
    mbr = jnp.concatenate(mparts, axis=1) * jax.nn.silu(seg(_MG, MEM_W))

    pc = jnp.dot(cbr.astype(BF16), wb1_ref[...], preferred_element_type=F32)
    pm = jnp.dot(mbr.astype(BF16), wb2_ref[...], preferred_element_type=F32)
    g0_out[...] = jax.nn.sigmoid(seg(_MERGE, D_MODEL))
    part_out[...] = (jax.nn.sigmoid(seg(_MERGE + D_MODEL, D_MODEL)) * pc
                     + jax.nn.sigmoid(seg(_MERGE + 2 * D_MODEL, D_MODEL)) * pm)

    _page_copies(cache_ref, pt_ref, page_buf, page_sem, layer, sb, chunk, start=False)
    _sample_logits_step(qs_ref, kn_ref, page_buf, logit_scr, qb_scr, p_out, l_out, chunk,
                        dec_seq=dec_seq, n_pages=n_pages)


def _const_spec(shape, ngrid=1):
    zeros = (0,) * len(shape)
    if ngrid == 1:
        return pl.BlockSpec(shape, lambda i: zeros)
    return pl.BlockSpec(shape, lambda i, j: zeros)


def _proj_prompt(page_table, x2d, g, w_in_b, tabs, cw, cb, lng, lnb, mk, mv, w_br_b, q_s, k_s, cache_kt,
                 kv_prev, layer, batch, seq, dec_seq):
    tm = ROW_TILE
    n = batch * seq
    tps = seq // tm
    nt = n // tm
    dec_batch, n_pages = page_table.shape
    sps = n_pages // FUSED_PAGES
    assert nt == dec_batch * sps
    rows_s = N_HEADS * dec_seq
    n_prev = 0 if kv_prev is None else kv_prev[0].shape[0]
    const = lambda shape: pl.BlockSpec(shape, lambda t, pt: (0,) * len(shape))
    row = lambda w: pl.BlockSpec((tm, w), lambda t, pt: (t, 0))
    tab = pl.BlockSpec((tm, LANES), lambda t, pt: (t % tps, 0))
    mem = pl.BlockSpec((None, MEM_LEN, MEM_W), lambda t, pt: (layer, t // tps, 0))
    wbr = lambda br: pl.BlockSpec((None, None, ATTN_W, D_MODEL), lambda t, pt: (layer, br, 0, 0))
    seq_rows = pl.BlockSpec((dec_seq, ATTN_W), lambda t, pt: (t // sps, 0))
    in_specs = [
        row(D_MODEL), const((1, D_MODEL)),
        pl.BlockSpec((None, D_MODEL, N_IN), lambda t, pt: (layer, 0, 0), pipeline_mode=pl.Buffered(1)),
        tab, tab, tab,
        const((CONV_K, CONV_CH)), const((1, CONV_CH)), const((1, CONV_CH)), const((1, CONV_CH)),
        mem, mem, wbr(1), wbr(2),
        seq_rows, seq_rows, pl.BlockSpec(memory_space=pl.ANY),
    ]
    args = [page_table, x2d, g, w_in_b, *tabs, cw, cb, lng, lnb, mk, mv, w_br_b, w_br_b, q_s, k_s, cache_kt]
    if n_prev:
        prev = pl.BlockSpec((n_prev, None, ATTN_W, tm), lambda t, pt: (0, t // tps, 0, t % tps))
        in_specs += [prev, prev]
        args += list(kv_prev)
    out_shape = [
        jax.ShapeDtypeStruct((n, ATTN_W), F32),
        jax.ShapeDtypeStruct((n_prev + 1, batch, ATTN_W, seq), F32),
        jax.ShapeDtypeStruct((n_prev + 1, batch, ATTN_W, seq), F32),
        jax.ShapeDtypeStruct((nt, tm, ATTN_W), BF16),
        jax.ShapeDtypeStruct((nt, ATTN_W, tm), BF16),
        jax.ShapeDtypeStruct((nt, 1, ATTN_W), F32),
        jax.ShapeDtypeStruct((n, ATTN_W), F32),
        jax.ShapeDtypeStruct((n, D_MODEL), F32),
        jax.ShapeDtypeStruct((n, D_MODEL), F32),
        jax.ShapeDtypeStruct((batch, CONV_HALO, CONV_CH), F32),
        jax.ShapeDtypeStruct((dec_batch, n_pages + 1, rows_s, PAGE_SIZE), BF16),
        jax.ShapeDtypeStruct((dec_batch, rows_s, LANES), F32),
    ]
    seq_t = pl.BlockSpec((n_prev + 1, None, ATTN_W, tm), lambda t, pt: (0, t // tps, 0, t % tps))
    out_specs = [
        row(ATTN_W), seq_t, seq_t,
        pl.BlockSpec((None, tm, ATTN_W), lambda t, pt: (t, 0, 0)),
        pl.BlockSpec((None, ATTN_W, tm), lambda t, pt: (t, 0, 0)),
        pl.BlockSpec((None, 1, ATTN_W), lambda t, pt: (t, 0, 0)),
        row(ATTN_W), row(D_MODEL), row(D_MODEL),
        pl.BlockSpec((None, CONV_HALO, CONV_CH), lambda t, pt: (t // tps, 0, 0)),
        pl.BlockSpec((None, n_pages + 1, rows_s, PAGE_SIZE), lambda t, pt: (t // sps, 0, 0, 0)),
        pl.BlockSpec((None, rows_s, LANES), lambda t, pt: (t // sps, 0, 0)),
    ]
    return pl.pallas_call(
        functools.partial(_proj_prompt_kernel, tiles_per_seq=tps, n_prev=n_prev, layer=layer,
                          dec_seq=dec_seq, n_pages=n_pages),
        grid_spec=pltpu.PrefetchScalarGridSpec(
            num_scalar_prefetch=1,
            grid=(nt,),
            in_specs=in_specs,
            out_specs=out_specs,
            scratch_shapes=[pltpu.VMEM((tm + HALO_PAD, CONV_CH), F32),
                            pltpu.VMEM((SUBLANES - 1, tm + HALO_PAD - SUBLANES, CONV_CH), F32),
                            pltpu.VMEM((FUSED_PAGES, N_HEADS, HEAD_DIM, PAGE_SIZE), F32),
                            pltpu.SemaphoreType.DMA((FUSED_PAGES,)),
                            pltpu.VMEM((n_pages, rows_s, PAGE_SIZE), F32),
                            pltpu.VMEM((rows_s, ATTN_W), BF16)]),
        out_shape=out_shape,
        compiler_params=pltpu.CompilerParams(dimension_semantics=("arbitrary",),
                                             vmem_limit_bytes=VMEM_LIMIT),
        name="proj_prompt",
    )(*args)


def _attn_prompt_kernel(pt_ref, q_ref, kb_ref, vt_ref, km_ref, sa_ref, g0_ref, part_ref, x_ref, wb0_ref, wout_ref,
                        gf_ref, ps_ref, pns_ref, ls_ref, vn_ref, cache_ref, o_ref, so_ref,
                        sel_scr, qb_scr, m_scr, l_scr, acc_scr, page_buf, page_sem, sacc_scr,
                        *, final, nblk, layer, dec_seq, n_pages):
    tq = ROW_TILE
    i = pl.program_id(1)
    step = pl.program_id(0) * nblk + i
    steps_per_seq = n_pages // FUSED_PAGES
    sb, chunk = step // steps_per_seq, step % steps_per_seq
    _page_copies(cache_ref, pt_ref, page_buf, page_sem, layer, sb, chunk, start=True)
    blk_idx = lax.broadcasted_iota(jnp.int32, (nblk, tq), 0)
    valid = blk_idx < i
    lane = lax.broadcasted_iota(jnp.int32, (tq, LANES), 1)
    causal = (lax.broadcasted_iota(jnp.int32, (MOBA_BLOCK, tq), 0)
              <= lax.broadcasted_iota(jnp.int32, (MOBA_BLOCK, tq), 1))
    pair = lambda h: slice((h // 2) * LANES, (h // 2 + 1) * LANES)
    rows = lambda h: slice(h * HEAD_DIM, (h + 1) * HEAD_DIM)
    bcast = lambda r: jnp.broadcast_to(r, (SUBLANES, tq))

    for h in range(N_HEADS):
        qm = jnp.where((lane // HEAD_DIM) == h % 2, q_ref[:, pair(h)], 0.0)
        sc = lax.dot_general(km_ref[:, pair(h)], qm, _NT, precision=lax.Precision.HIGHEST,
                             preferred_element_type=F32)
        sel_scr[h] = jnp.where(_top_k_select(sc, valid, blk_idx, nblk), 1.0, 0.0)
        qmb = (qm * (ATTN_SCALE * LOG2E)).astype(BF16)
        qb_scr[h] = qmb
        s = lax.dot_general(kb_ref[i, :, pair(h)], qmb, _NT, preferred_element_type=F32)
        s = jnp.where(causal, s, -jnp.inf)
        m0 = jnp.max(s, axis=0, keepdims=True)
        p = jnp.exp2(s - m0)
        m_scr[h] = bcast(m0)
        l_scr[h] = bcast(jnp.sum(p, axis=0, keepdims=True))
        acc_scr[rows(h), :] = jnp.dot(vt_ref[i, rows(h), :], p.astype(BF16), preferred_element_type=F32)

    def logits(j, h):
        return lax.dot_general(kb_ref[j, :, pair(h)], qb_scr[h], _NT, preferred_element_type=F32)

    def body(j, ahead):
        ahead = list(ahead)
        for h in range(N_HEADS):
            s = ahead.pop(0)
            nh = h + LOGITS_AHEAD
            ahead.append(logits(j, nh) if nh < N_HEADS else logits(j + 1, nh - N_HEADS))
            s = jnp.where(sel_scr[h, pl.ds(j, 1), :] > 0.0, s, -jnp.inf)
            m_old = m_scr[h, 0:1, :]
            m_new = jnp.maximum(m_old, jnp.max(s, axis=0, keepdims=True))
            alpha = jnp.exp2(m_old - m_new)
            p = jnp.exp2(s - m_new)
            m_scr[h] = bcast(m_new)
            l_scr[h] = bcast(alpha * l_scr[h, 0:1, :] + jnp.sum(p, axis=0, keepdims=True))
            acc_scr[rows(h), :] = alpha * acc_scr[rows(h), :] + jnp.dot(
                vt_ref[j, rows(h), :], p.astype(BF16), preferred_element_type=F32)
        return tuple(ahead)

    lax.fori_loop(0, i, body, tuple(logits(0, h) for h in range(LOGITS_AHEAD)))
    for h in range(N_HEADS):
        acc_scr[rows(h), :] = acc_scr[rows(h), :] / l_scr[h, 0:1, :]

    attn = acc_scr[...].T
    xn = _merge_out(attn, sa_ref[...], g0_ref[...], part_ref[...], x_ref[...], wb0_ref[...], wout_ref[...])
    if final:
        xn = _rms_norm(xn, gf_ref[...])
    o_ref[...] = xn

    _page_copies(cache_ref, pt_ref, page_buf, page_sem, layer, sb, chunk, start=False)
    _sample_values_step(ps_ref, pns_ref, ls_ref, vn_ref, page_buf, sacc_scr, so_ref, chunk,
                        dec_seq=dec_seq, n_chunks=steps_per_seq)


def _attn_prompt(page_table, q, kb, vt, km, sa, g0, part, x2d, w_br_b, w_out_b, gf, p_s, l_s, v_s, cache_vt,
                 layer, batch, seq, dec_seq, final):
    tq = ROW_TILE
    nblk = seq // MOBA_BLOCK
    dec_batch, n_pages = page_table.shape
    sps = n_pages // FUSED_PAGES
    assert batch * nblk == dec_batch * sps
    rows_s = N_HEADS * dec_seq
    sstep = lambda b, i: (b * nblk + i) // sps
    row = lambda w: pl.BlockSpec((tq, w), lambda b, i, pt: (b * nblk + i, 0))
    seq_rows = pl.BlockSpec((dec_seq, ATTN_W), lambda b, i, pt: (sstep(b, i), 0))
    in_specs = [
        row(ATTN_W),
        pl.BlockSpec((nblk, MOBA_BLOCK, ATTN_W), lambda b, i, pt: (b, 0, 0)),
        pl.BlockSpec((nblk, ATTN_W, MOBA_BLOCK), lambda b, i, pt: (b, 0, 0)),
        pl.BlockSpec((None, nblk, ATTN_W), lambda b, i, pt: (b, 0, 0)),
        row(ATTN_W), row(D_MODEL), row(D_MODEL), row(D_MODEL),
        pl.BlockSpec((None, None, ATTN_W, D_MODEL), lambda b, i, pt: (layer, 0, 0, 0)),
        pl.BlockSpec((None, D_MODEL, D_MODEL), lambda b, i, pt: (layer, 0, 0)),
        pl.BlockSpec((1, D_MODEL), lambda b, i, pt: (0, 0)),
        pl.BlockSpec((None, FUSED_PAGES, rows_s, PAGE_SIZE),
                     lambda b, i, pt: (sstep(b, i), (b * nblk + i) % sps, 0, 0)),
        pl.BlockSpec((None, None, rows_s, PAGE_SIZE), lambda b, i, pt: (sstep(b, i), n_pages, 0, 0)),
        pl.BlockSpec((None, rows_s, LANES), lambda b, i, pt: (sstep(b, i), 0, 0)),
        seq_rows, pl.BlockSpec(memory_space=pl.ANY),
    ]
    return pl.pallas_call(
        functools.partial(_attn_prompt_kernel, final=final, nblk=nblk, layer=layer, dec_seq=dec_seq,
                          n_pages=n_pages),
        grid_spec=pltpu.PrefetchScalarGridSpec(
            num_scalar_prefetch=1,
            grid=(batch, nblk),
            in_specs=in_specs,
            out_specs=[row(D_MODEL), seq_rows],
            scratch_shapes=[pltpu.VMEM((N_HEADS, nblk, tq), F32),
                            pltpu.VMEM((N_HEADS, tq, LANES), BF16),
                            pltpu.VMEM((N_HEADS, SUBLANES, tq), F32),
                            pltpu.VMEM((N_HEADS, SUBLANES, tq), F32),
                            pltpu.VMEM((ATTN_W, tq), F32),
                            pltpu.VMEM((FUSED_PAGES, N_HEADS, HEAD_DIM, PAGE_SIZE), F32),
                            pltpu.SemaphoreType.DMA((FUSED_PAGES,)),
                            pltpu.VMEM((rows_s, ATTN_W), F32)]),
        out_shape=[jax.ShapeDtypeStruct((batch * seq, D_MODEL), F32),
                   jax.ShapeDtypeStruct((dec_batch * dec_seq, ATTN_W), F32)],
        compiler_params=pltpu.CompilerParams(dimension_semantics=("arbitrary", "arbitrary"),
                                             vmem_limit_bytes=VMEM_LIMIT),
        name="attn_prompt",
    )(page_table, q, kb, vt, km, sa, g0, part, x2d, w_br_b, w_out_b, gf, p_s, p_s, l_s, v_s, cache_vt)


def _proj_sample_kernel(x_ref, g_ref, w_ref, cos_ref, sina_ref, sinb_ref, cw_ref, cb_ref, lng_ref, lnb_ref,
                        st_ref, mk_ref, mv_ref, wb1_ref, wb2_ref,
                        q_out, k_out, v_out, sa_out, g0_out, part_out, cst_out,
                        u_scr, cgate_scr, mq_scr, mgate_scr, g12_scr, c_scr, m_scr, full_scr, *, dec_seq):
    b = pl.program_id(0)
    nb = pl.num_programs(0)

    @pl.when(b == 0)
    def _():
        hb = _rms_norm(x_ref[...], g_ref[...]).astype(BF16)

        def seg(a, width):
            return jnp.dot(hb, w_ref[:, a:a + width], preferred_element_type=F32)

        cosf, sina, sinb = cos_ref[...], sina_ref[...], sinb_ref[...]
        zq = seg(_Q, ATTN_W)
        zk = seg(_K, ATTN_W)
        for c in range(ATTN_W // LANES):
            sl = slice(c * LANES, (c + 1) * LANES)
            q_out[:, sl] = _rope(zq[:, sl], cosf, sina, sinb)
            k_out[:, sl] = _rope(zk[:, sl], cosf, sina, sinb)
        v_out[...] = seg(_V, ATTN_W)
        sa_out[...] = jax.nn.silu(seg(_AG, ATTN_W))
        u_scr[...] = seg(_CV, CONV_CH) * jax.nn.sigmoid(seg(_CG, CONV_CH))
        cgate_scr[...] = jax.nn.silu(seg(_CGATE, CONV_CH))
        mq_scr[...] = seg(_MQ, MEM_W)
        mgate_scr[...] = jax.nn.silu(seg(_MG, MEM_W))
        g0_out[...] = jax.nn.sigmoid(seg(_MERGE, D_MODEL))
        g12_scr[:, 0:D_MODEL] = jax.nn.sigmoid(seg(_MERGE + D_MODEL, D_MODEL))
        g12_scr[:, D_MODEL:2 * D_MODEL] = jax.nn.sigmoid(seg(_MERGE + 2 * D_MODEL, D_MODEL))

    r0 = pl.multiple_of(b * dec_seq, dec_seq)
    rows = pl.ds(r0, dec_seq)

    full_scr[0:CONV_HALO, :] = st_ref[...]
    full_scr[CONV_HALO:CONV_HALO + dec_seq, :] = u_scr[rows, :]
    conv = jnp.broadcast_to(cb_ref[...], (dec_seq, CONV_CH))
    for t in range(CONV_K):
        conv = conv + cw_ref[t:t + 1, :] * full_scr[t:t + dec_seq, :]
    cst_out[...] = full_scr[dec_seq:dec_seq + CONV_HALO, :]
    c_scr[rows, :] = jax.nn.silu(_layer_norm(conv, lng_ref[...], lnb_ref[...])) * cgate_scr[rows, :]

    mq = mq_scr[rows, :]
    mparts = []
    for hd in range(MEM_HEADS):
        sl = slice(hd * MEM_HEAD_DIM, (hd + 1) * MEM_HEAD_DIM)
        s = lax.dot_general(mq[:, sl], mk_ref[:, hd, :], _NT, preferred_element_type=F32) * MEM_SCALE
        p = jnp.exp(s - jnp.max(s, axis=-1, keepdims=True))
        o = jnp.dot(p, mv_ref[:, hd, :], preferred_element_type=F32)
        mparts.append(o / jnp.sum(p, axis=-1, keepdims=True))
    m_scr[rows, :] = jnp.concatenate(mparts, axis=1) * mgate_scr[rows, :]

    @pl.when(b == nb - 1)
    def _():
        pc = jnp.dot(c_scr[...].astype(BF16), wb1_ref[...], preferred_element_type=F32)
        pm = jnp.dot(m_scr[...].astype(BF16), wb2_ref[...], preferred_element_type=F32)
        part_out[...] = g12_scr[:, 0:D_MODEL] * pc + g12_scr[:, D_MODEL:2 * D_MODEL] * pm


def _proj_sample(x2d, g, w_in_b, tabs, cw, cb, lng, lnb, state_conv, cache_mem_k, cache_mem_v, w_br_b,
                 layer, dec_batch, dec_seq):
    n = dec_batch * dec_seq
    full = lambda w: _const_spec((n, w))
    mem = pl.BlockSpec((None, None, MEM_LEN, MEM_HEADS, MEM_HEAD_DIM), lambda b: (layer, b, 0, 0, 0))
    wbr = lambda br: pl.BlockSpec((None, None, ATTN_W, D_MODEL), lambda b: (layer, br, 0, 0))
    in_specs = [
        full(D_MODEL), _const_spec((1, D_MODEL)),
        pl.BlockSpec((None, D_MODEL, N_IN), lambda b: (layer, 0, 0), pipeline_mode=pl.Buffered(1)),
        full(LANES), full(LANES), full(LANES),
        _const_spec((CONV_K, CONV_CH)), _const_spec((1, CONV_CH)), _const_spec((1, CONV_CH)),
        _const_spec((1, CONV_CH)),
        pl.BlockSpec((None, None, CONV_HALO, CONV_CH), lambda b: (layer, b, 0, 0)),
        mem, mem, wbr(1), wbr(2),
    ]
    out_shape = [jax.ShapeDtypeStruct((n, ATTN_W), F32)] * 4 + [
        jax.ShapeDtypeStruct((n, D_MODEL), F32), jax.ShapeDtypeStruct((n, D_MODEL), F32),
        jax.ShapeDtypeStruct((dec_batch, CONV_HALO, CONV_CH), F32)]
    out_specs = [full(ATTN_W)] * 4 + [full(D_MODEL), full(D_MODEL),
                                      pl.BlockSpec((None, CONV_HALO, CONV_CH), lambda b: (b, 0, 0))]
    scr = lambda w: pltpu.VMEM((n, w), F32)
    return pl.pallas_call(
        functools.partial(_proj_sample_kernel, dec_seq=dec_seq),
        grid=(dec_batch,),
        in_specs=in_specs,
        out_specs=out_specs,
        out_shape=out_shape,
        scratch_shapes=[scr(CONV_CH), scr(CONV_CH), scr(MEM_W), scr(MEM_W), scr(2 * D_MODEL),
                        scr(CONV_CH), scr(MEM_W),
                        pltpu.VMEM((CONV_HALO + dec_seq + 2, CONV_CH), F32)],
        compiler_params=pltpu.CompilerParams(dimension_semantics=("arbitrary",),
                                             vmem_limit_bytes=VMEM_LIMIT),
        name="proj_sample",
    )(x2d, g, w_in_b, *tabs, cw, cb, lng, lnb, state_conv, cache_mem_k, cache_mem_v, w_br_b, w_br_b)


def _page_copies(cache_ref, pt_ref, page_buf, page_sem, layer, sb, chunk, *, start):
    for r in range(FUSED_PAGES):
        page = pt_ref[sb, chunk * FUSED_PAGES + r]
        copy = pltpu.make_async_copy(cache_ref.at[layer, page], page_buf.at[r], page_sem.at[r])
        if start:
            copy.start()
        else:
            copy.wait()


def _head_rows(x, dec_seq):
    rows = N_HEADS * dec_seq
    tiled = jnp.concatenate([x] * N_HEADS, axis=0)
    row_h = lax.broadcasted_iota(jnp.int32, (rows, ATTN_W), 0) // dec_seq
    lane_h = lax.broadcasted_iota(jnp.int32, (rows, ATTN_W), 1) // HEAD_DIM
    return jnp.where(row_h == lane_h, tiled, 0.0)


def _sample_logits_step(q_ref, kn_ref, page_buf, logit_scr, qb_scr, p_out, l_out, c, *, dec_seq, n_pages):
    nc = n_pages // FUSED_PAGES
    rows = N_HEADS * dec_seq
    pages_per_blk = MOBA_BLOCK // PAGE_SIZE
    nblk = n_pages // pages_per_blk

    @pl.when(c == 0)
    def _():
        qb_scr[...] = (_head_rows(q_ref[...], dec_seq) * ATTN_SCALE).astype(BF16)

    qb = qb_scr[...]
    for r in range(FUSED_PAGES):
        kt = page_buf[r].reshape(ATTN_W, PAGE_SIZE)
        logit_scr[c * FUSED_PAGES + r] = jnp.dot(qb, kt.astype(BF16), preferred_element_type=F32)

    @pl.when(c == nc - 1)
    def _():
        blk_idx = lax.broadcasted_iota(jnp.int32, (rows, nblk), 1)
        sc = jnp.zeros((rows, nblk), F32)
        for j in range(nblk):
            blk = logit_scr[j * pages_per_blk]
            for pp in range(1, pages_per_blk):
                blk = blk + logit_scr[j * pages_per_blk + pp]
            sc = jnp.where(blk_idx == j, jnp.sum(blk, axis=-1, keepdims=True), sc)
        sel = _top_k_select(sc, blk_idx >= 0, blk_idx, nblk, axis=1)
        sel_t = jnp.where(sel, 1.0, 0.0)

        kn_page = jnp.concatenate([kn_ref[...], jnp.zeros((PAGE_SIZE - dec_seq, ATTN_W), F32)], axis=0)
        ln = lax.dot_general(qb, kn_page.astype(BF16), _NT, preferred_element_type=F32)
        key_i = lax.broadcasted_iota(jnp.int32, (rows, PAGE_SIZE), 1)
        qry_i = lax.broadcasted_iota(jnp.int32, (rows, PAGE_SIZE), 0) % dec_seq
        ln = jnp.where(key_i <= qry_i, ln, -jnp.inf)

        mx = ln
        for p in range(n_pages):
            j = p // pages_per_blk
            lp = jnp.where(sel_t[:, j:j + 1] > 0.0, logit_scr[p], -jnp.inf)
            logit_scr[p] = lp
            mx = jnp.maximum(mx, lp)
        m = jnp.max(mx, axis=-1, keepdims=True)
        pn = jnp.exp(ln - m)
        p_out[n_pages] = pn.astype(BF16)
        lsum = pn
        for p in range(n_pages):
            pp = jnp.exp(logit_scr[p] - m)
            p_out[p] = pp.astype(BF16)
            lsum = lsum + pp
        l_out[...] = jnp.broadcast_to(jnp.sum(lsum, axis=-1, keepdims=True), (rows, LANES))


def _sample_values_step(p_ref, pn_ref, l_ref, vn_ref, page_buf, acc_scr, o_ref, c, *, dec_seq, n_chunks):
    @pl.when(c == 0)
    def _():
        vn_page = jnp.concatenate([vn_ref[...], jnp.zeros((PAGE_SIZE - dec_seq, ATTN_W), F32)], axis=0)
        acc_scr[...] = jnp.dot(pn_ref[...], vn_page.astype(BF16), preferred_element_type=F32)

    acc = acc_scr[...]
    for r in range(FUSED_PAGES):
        vt = page_buf[r].reshape(ATTN_W, PAGE_SIZE)
        acc = acc + lax.dot_general(p_ref[r], vt.astype(BF16), _NT, preferred_element_type=F32)
    acc_scr[...] = acc

    @pl.when(c == n_chunks - 1)
    def _():
        res = acc / jnp.concatenate([l_ref[...]] * (ATTN_W // LANES), axis=1)
        lane_h = lax.broadcasted_iota(jnp.int32, (dec_seq, ATTN_W), 1) // HEAD_DIM
        out = jnp.zeros((dec_seq, ATTN_W), F32)
        for h in range(N_HEADS):
            out = out + jnp.where(lane_h == h, res[h * dec_seq:(h + 1) * dec_seq, :], 0.0)
        o_ref[...] = out


def _merge_sample_kernel(a_ref, sa_ref, g0_ref, part_ref, x_ref, wb0_ref, wout_ref, gf_ref, o_ref, *, final):
    xn = _merge_out(a_ref[...], sa_ref[...], g0_ref[...], part_ref[...], x_ref[...], wb0_ref[...],
                    wout_ref[...])
    if final:
        xn = _rms_norm(xn, gf_ref[...])
    o_ref[...] = xn


def _merge_sample(attn, sa, g0, part, x2d, w_br_b, w_out_b, gf, layer, final):
    n = x2d.shape[0]
    full = lambda w: _const_spec((n, w))
    return pl.pallas_call(
        functools.partial(_merge_sample_kernel, final=final),
        grid=(1,),
        in_specs=[full(ATTN_W), full(ATTN_W), full(D_MODEL), full(D_MODEL), full(D_MODEL),
                  pl.BlockSpec((None, None, ATTN_W, D_MODEL), lambda i: (layer, 0, 0, 0)),
                  pl.BlockSpec((None, D_MODEL, D_MODEL), lambda i: (layer, 0, 0)), _const_spec((1, D_MODEL))],
        out_specs=full(D_MODEL),
        out_shape=jax.ShapeDtypeStruct((n, D_MODEL), F32),
        compiler_params=pltpu.CompilerParams(dimension_semantics=("arbitrary",),
                                             vmem_limit_bytes=VMEM_LIMIT),
        name="merge_sample",
    )(attn, sa, g0, part, x2d, w_br_b, w_out_b, gf)


def kernel(x_prompt, x_sample, cache_k, cache_v, cache_mem_k, cache_mem_v, state_conv, page_table, mem_prompt,
           g_norm, w_in, conv_w, conv_b, ln_g, ln_b, w_mem_k, w_mem_v, w_branch, w_out, g_final):
    batch, seq, _ = x_prompt.shape
    dec_batch, dec_seq, _ = x_sample.shape
    depth = w_in.shape[0]
    past_len = page_table.shape[1] * PAGE_SIZE
    assert seq % ROW_TILE == 0 and ROW_TILE == MOBA_BLOCK
    assert dec_batch * dec_seq == ROW_TILE and dec_seq == 8 and past_len % MOBA_BLOCK == 0
    assert page_table.shape[1] % FUSED_PAGES == 0

    w_in_b = w_in.astype(BF16)
    w_br_b = w_branch.astype(BF16)
    w_out_b = w_out.astype(BF16)
    gf = g_final.reshape(1, D_MODEL)

    mem_k_p, mem_v_p = _mem_proj(mem_prompt.reshape(batch * MEM_LEN, D_MODEL), w_mem_k, w_mem_v)

    tabs_p = _rope_tables(jnp.arange(seq, dtype=jnp.int32))
    tabs_s = tuple(jnp.tile(t, (dec_batch, 1))
                   for t in _rope_tables(past_len + jnp.arange(dec_seq, dtype=jnp.int32)))

    cache_kt = cache_k.transpose(0, 1, 3, 4, 2)
    cache_vt = cache_v.transpose(0, 1, 3, 4, 2)

    xp = x_prompt.reshape(batch * seq, D_MODEL)
    xs = x_sample.reshape(dec_batch * dec_seq, D_MODEL)
    kv_p = None
    cp_l, ks_l, vs_l, cs_l = [], [], [], []
    for l in range(depth):
        g = g_norm[l].reshape(1, D_MODEL)
        cw, cb = conv_w[l], conv_b[l].reshape(1, CONV_CH)
        lng, lnb = ln_g[l].reshape(1, CONV_CH), ln_b[l].reshape(1, CONV_CH)
        final = l == depth - 1

        qs, k_s, v_s, sas, g0s, parts, csts = _proj_sample(
            xs, g, w_in_b, tabs_s, cw, cb, lng, lnb, state_conv, cache_mem_k, cache_mem_v, w_br_b,
            l, dec_batch, dec_seq)
        q, kt, vt, kb, vtb, km, sa, g0, part, cst, p_s, l_s = _proj_prompt(
            page_table, xp, g, w_in_b, tabs_p, cw, cb, lng, lnb, mem_k_p, mem_v_p, w_br_b, qs, k_s, cache_kt,
            kv_p, l, batch, seq, dec_seq)
        xp, attn_s = _attn_prompt(
            page_table, q, kb, vtb, km.reshape(batch, seq // MOBA_BLOCK, ATTN_W), sa, g0, part, xp,
            w_br_b, w_out_b, gf, p_s, l_s, v_s, cache_vt, l, batch, seq, dec_seq, final)
        kv_p = (kt, vt)
        cp_l.append(cst)
        xs = _merge_sample(attn_s, sas, g0s, parts, xs, w_br_b, w_out_b, gf, l, final)
        ks_l.append(k_s); vs_l.append(v_s); cs_l.append(csts)

    y_prompt = xp.reshape(batch, seq, D_MODEL)
    y_sample = xs.reshape(dec_batch, dec_seq, D_MODEL)
    head_p = lambda t: t.reshape(depth, batch, N_HEADS, HEAD_DIM, seq).transpose(0, 1, 4, 2, 3)
    head_s = lambda ts: jnp.stack(ts).reshape(depth, dec_batch, dec_seq, N_HEADS, HEAD_DIM)
    mem_shape = (depth, batch, MEM_LEN, MEM_HEADS, MEM_HEAD_DIM)
    return (y_prompt, y_sample, head_p(kv_p[0]), head_p(kv_p[1]), jnp.stack(cp_l),
            mem_k_p.reshape(mem_shape), mem_v_p.reshape(mem_shape),
            head_s(ks_l), head_s(vs_l), jnp.stack(cs_l))
```

```python
import functools

import jax
import jax.numpy as jnp
from jax import lax
from jax.experimental import pallas as pl
from jax.experimental.pallas import tpu as pltpu

F32 = jnp.float32
BF16 = jnp.bfloat16

D_MODEL = 1024
N_HEADS = 8
HEAD_DIM = 64
ATTN_W = N_HEADS * HEAD_DIM
ROT_DIM = HEAD_DIM // 4
ROPE_THETA = 500000.0
MOBA_BLOCK = 256
MOBA_TOP_K = 3
CONV_CH = 512
CONV_K = 31
CONV_HALO = CONV_K - 1
MEM_LEN = 256
MEM_HEADS = 4
MEM_HEAD_DIM = 128
MEM_W = MEM_HEADS * MEM_HEAD_DIM
PAGE_SIZE = 128
RMS_EPS = 1e-6
LN_EPS = 1e-5
ATTN_SCALE = HEAD_DIM ** -0.5
MEM_SCALE = MEM_HEAD_DIM ** -0.5

LANES = 128
SUBLANES = 8
LOG2E = 1.4426950408889634
MASK_BIAS = -1e30
MAX_FLOOR = -1e29
HEAD_ROWS = HEAD_DIM + 16
ROW_TILE = 256
LOGITS_AHEAD = 4
FUSED_PAGES = 32
PAGE_GROUP = 2
VALUE_GROUP = 8
HALO_PAD = 32
VMEM_LIMIT = 56 * 1024 * 1024

_Q, _K, _V, _AG, _CV, _CG, _CGATE, _MQ, _MG, _MERGE = (
    0, 512, 1024, 1536, 2048, 2560, 3072, 3584, 4096, 4608)
N_IN = _MERGE + 3 * D_MODEL

_NT = (((1,), (1,)), ((), ()))


def _rms_norm(x, g):
    return x * lax.rsqrt(jnp.mean(x * x, axis=-1, keepdims=True) + RMS_EPS) * g


def _layer_norm(x, g, b):
    mu = jnp.mean(x, axis=-1, keepdims=True)
    xc = x - mu
    var = jnp.mean(xc * xc, axis=-1, keepdims=True)
    return xc * lax.rsqrt(var + LN_EPS) * g + b


def _rope(xc, cosf, sina, sinb):
    return (xc * cosf + pltpu.roll(xc, LANES - ROT_DIM // 2, 1) * sina
            + pltpu.roll(xc, ROT_DIM // 2, 1) * sinb)


def _rope_tables(pos):
    half = ROT_DIM // 2
    inv = ROPE_THETA ** (-jnp.arange(0, ROT_DIM, 2, dtype=F32) / ROT_DIM)
    ang = pos.astype(F32)[:, None] * inv[None, :]
    cos, sin = jnp.cos(ang), jnp.sin(ang)
    n = pos.shape[0]
    zeros_h = jnp.zeros((n, half), F32)
    rest0 = jnp.zeros((n, HEAD_DIM - ROT_DIM), F32)
    cosf = jnp.concatenate([cos, cos, jnp.ones((n, HEAD_DIM - ROT_DIM), F32)], axis=1)
    sina = jnp.concatenate([-sin, zeros_h, rest0], axis=1)
    sinb = jnp.concatenate([zeros_h, sin, rest0], axis=1)
    rep = LANES // HEAD_DIM
    return tuple(jnp.tile(t, (1, rep)) for t in (cosf, sina, sinb))


def _top_k_select(sc, valid, idx, n, axis=0):
    scm = jnp.where(valid, sc, -jnp.inf)
    rank = jnp.zeros(sc.shape, F32)
    for j in range(n):
        sj = scm[j:j + 1, :] if axis == 0 else scm[:, j:j + 1]
        beats = (sj > scm) | ((sj == scm) & (j < idx))
        rank = rank + jnp.where(beats, 1.0, 0.0)
    return valid & (rank < float(MOBA_TOP_K))


def _block_lane(h):
    return HEAD_DIM if h % 2 == 0 else 0


def _split_bf16(x):
    hi = x.astype(BF16)
    return hi, (x - hi.astype(F32)).astype(BF16)


def _merge_out(attn, sa, g0, part, x, wb0, wout):
    a = (attn * sa).astype(BF16)
    pa = jnp.dot(a, wb0, preferred_element_type=F32)
    mix = (g0 * pa + part).astype(BF16)
    return x + jnp.dot(mix, wout, preferred_element_type=F32)


def _mem_proj_kernel(mem_ref, wk_ref, wv_ref, mk_out, mv_out):
    mb = mem_ref[...].astype(BF16)
    mk_out[...] = jnp.dot(mb, wk_ref[...].astype(BF16), preferred_element_type=F32)
    mv_out[...] = jnp.dot(mb, wv_ref[...].astype(BF16), preferred_element_type=F32)


def _mem_proj(mem2d, w_mem_k, w_mem_v):
    depth = w_mem_k.shape[0]
    rows = mem2d.shape[0]
    w_spec = pl.BlockSpec((None, D_MODEL, MEM_W), lambda l: (l, 0, 0))
    o_spec = pl.BlockSpec((None, rows, MEM_W), lambda l: (l, 0, 0))
    return pl.pallas_call(
        _mem_proj_kernel,
        grid=(depth,),
        in_specs=[pl.BlockSpec((rows, D_MODEL), lambda l: (0, 0)), w_spec, w_spec],
        out_specs=[o_spec, o_spec],
        out_shape=[jax.ShapeDtypeStruct((depth, rows, MEM_W), F32)] * 2,
        compiler_params=pltpu.CompilerParams(dimension_semantics=("arbitrary",),
                                             vmem_limit_bytes=VMEM_LIMIT),
        name="mem_proj",
    )(mem2d, w_mem_k, w_mem_v)


def _proj_prompt_kernel(pt_ref, x_ref, g_ref, w_ref, cos_ref, sina_ref, sinb_ref, cw_ref, cb_ref, lng_ref, lnb_ref,
                        mk_ref, mv_ref, wb1_ref, wb2_ref, qs_ref, kn_ref, cache_ref, *refs,
                        tiles_per_seq, n_prev, layer, dec_seq, n_pages):
    if n_prev:
        ktp_ref, vtp_ref = refs[:2]
        refs = refs[2:]
    (q_out, kt_out, vt_out, kb_out, vtb_out, km_out, sa_out, g0_out, part_out, cst_out, p_out, l_out,
     ubuf, urot, page_buf, page_sem, logit_scr, qb_scr) = refs
    tm = ROW_TILE
    step = pl.program_id(0)
    tin = step % tiles_per_seq
    steps_per_seq = n_pages // FUSED_PAGES
    sb, chunk = step // steps_per_seq, step % steps_per_seq
    _page_copies(cache_ref, pt_ref, page_buf, page_sem, layer, sb, chunk, start=True)
    if n_prev:
        kt_out[0:n_prev] = ktp_ref[...]
        vt_out[0:n_prev] = vtp_ref[...]
    hb = _rms_norm(x_ref[...], g_ref[...]).astype(BF16)

    def seg(a, width):
        return jnp.dot(hb, w_ref[:, a:a + width], preferred_element_type=F32)

    cosf, sina, sinb = cos_ref[...], sina_ref[...], sinb_ref[...]
    zq = seg(_Q, ATTN_W)
    zk = seg(_K, ATTN_W)
    for c in range(ATTN_W // LANES):
        sl = slice(c * LANES, (c + 1) * LANES)
        q_out[:, sl] = _rope(zq[:, sl], cosf, sina, sinb)
        kr = _rope(zk[:, sl], cosf, sina, sinb)
        kt_out[n_prev, sl, :] = kr.T
        lane = lax.broadcasted_iota(jnp.int32, (tm, LANES), 1)
        for hh in range(2):
            onehot = jnp.where(lane == _block_lane(2 * c + hh) + tin, 1.0, 0.0)
            kb_out[:, (2 * c + hh) * LANES:(2 * c + hh + 1) * LANES] = jnp.where(
                lane // HEAD_DIM == hh, kr, onehot).astype(BF16)
        km_out[:, sl] = jnp.mean(kr, axis=0, keepdims=True)
    zvt = seg(_V, ATTN_W).T
    vt_out[n_prev] = zvt
    for h in range(N_HEADS):
        vtb_out[h * HEAD_ROWS:h * HEAD_ROWS + HEAD_DIM, :] = zvt[h * HEAD_DIM:(h + 1) * HEAD_DIM, :].astype(BF16)
        vtb_out[h * HEAD_ROWS + HEAD_DIM:(h + 1) * HEAD_ROWS, :] = jnp.ones((HEAD_ROWS - HEAD_DIM, tm), BF16)
    sa_out[...] = jax.nn.silu(seg(_AG, ATTN_W))

    u = seg(_CV, CONV_CH) * jax.nn.sigmoid(seg(_CG, CONV_CH))

    @pl.when(tin == 0)
    def _():
        ubuf[0:HALO_PAD, :] = jnp.zeros((HALO_PAD, CONV_CH), F32)

    @pl.when(tin != 0)
    def _():
        ubuf[HALO_PAD - CONV_HALO:HALO_PAD, :] = ubuf[tm + HALO_PAD - CONV_HALO:tm + HALO_PAD, :]

    ubuf[HALO_PAD:HALO_PAD + tm, :] = u
    nrot = tm + HALO_PAD - SUBLANES
    for r in range(1, SUBLANES):
        urot[r - 1] = ubuf[r:r + nrot, :]
    cst_out[...] = ubuf[tm + HALO_PAD - CONV_HALO:tm + HALO_PAD, :]
    conv = jnp.broadcast_to(cb_ref[...], (tm, CONV_CH))
    for t in range(CONV_K):
        a, r = divmod(HALO_PAD - CONV_HALO + t, SUBLANES)
        rows = slice(a * SUBLANES, a * SUBLANES + tm)
        conv = conv + cw_ref[t:t + 1, :] * (ubuf[rows, :] if r == 0 else urot[r - 1, rows, :])
    cbr = jax.nn.silu(_layer_norm(conv, lng_ref[...], lnb_ref[...])) * jax.nn.silu(seg(_CGATE, CONV_CH))

    zmq = seg(_MQ, MEM_W)
    mparts = []
    for hd in range(MEM_HEADS):
        sl = slice(hd * MEM_HEAD_DIM, (hd + 1) * MEM_HEAD_DIM)
        s = lax.dot_general(zmq[:, sl].astype(BF16), mk_ref[:, sl].astype(BF16), _NT,
                            preferred_element_type=F32) * MEM_SCALE
        p = jnp.exp(s - jnp.max(s, axis=-1, keepdims=True))
        o = jnp.dot(p.astype(BF16), mv_ref[:, sl].astype(BF16), preferred_element_type=F32)
        mparts.append(o / jnp.sum(p, axis=-1, keepdims=True))
    mbr = jnp.concatenate(mparts, axis=1) * jax.nn.silu(seg(_MG, MEM_W))

    pc = jnp.dot(cbr.astype(BF16), wb1_ref[...], preferred_element_type=F32)
    pm = jnp.dot(mbr.astype(BF16), wb2_ref[...], preferred_element_type=F32)
    g0_out[...] = jax.nn.sigmoid(seg(_MERGE, D_MODEL))
    part_out[...] = (jax.nn.sigmoid(seg(_MERGE + D_MODEL, D_MODEL)) * pc
                     + jax.nn.sigmoid(seg(_MERGE + 2 * D_MODEL, D_MODEL)) * pm)

    _page_copies(cache_ref, pt_ref, page_buf, page_sem, layer, sb, chunk, start=False)
    _sample_logits_pages(qs_ref, page_buf, logit_scr, qb_scr, chunk, dec_seq=dec_seq)
    _sample_logits_finish(kn_ref, logit_scr, qb_scr, p_out, l_out, chunk, dec_seq=dec_seq, n_pages=n_pages)


def _const_spec(shape, ngrid=1):
    zeros = (0,) * len(shape)
    if ngrid == 1:
        return pl.BlockSpec(shape, lambda i: zeros)
    return pl.BlockSpec(shape, lambda i, j: zeros)


def _proj_prompt(page_table, x2d, g, w_in_b, tabs, cw, cb, lng, lnb, mk, mv, w_br_b, q_s, k_s, cache_kt,
                 kv_prev, layer, batch, seq, dec_seq):
    tm = ROW_TILE
    n = batch * seq
    tps = seq // tm
    nt = n // tm
    dec_batch, n_pages = page_table.shape
    sps = n_pages // FUSED_PAGES
    assert nt == dec_batch * sps
    rows_s = N_HEADS * dec_seq
    n_prev = 0 if kv_prev is None else kv_prev[0].shape[0]
    const = lambda shape: pl.BlockSpec(shape, lambda t, pt: (0,) * len(shape))
    row = lambda w: pl.BlockSpec((tm, w), lambda t, pt: (t, 0))
    tab = pl.BlockSpec((tm, LANES), lambda t, pt: (t % tps, 0))
    mem = pl.BlockSpec((None, MEM_LEN, MEM_W), lambda t, pt: (layer, t // tps, 0))
    wbr = lambda br: pl.BlockSpec((None, None, ATTN_W, D_MODEL), lambda t, pt: (layer, br, 0, 0))
    seq_rows = pl.BlockSpec((dec_seq, ATTN_W), lambda t, pt: (t // sps, 0))
    in_specs = [
        row(D_MODEL), const((1, D_MODEL)),
        pl.BlockSpec((None, D_MODEL, N_IN), lambda t, pt: (layer, 0, 0), pipeline_mode=pl.Buffered(1)),
        tab, tab, tab,
        const((CONV_K, CONV_CH)), const((1, CONV_CH)), const((1, CONV_CH)), const((1, CONV_CH)),
        mem, mem, wbr(1), wbr(2),
        seq_rows, seq_rows, pl.BlockSpec(memory_space=pl.ANY),
    ]
    args = [page_table, x2d, g, w_in_b, *tabs, cw, cb, lng, lnb, mk, mv, w_br_b, w_br_b, q_s, k_s, cache_kt]
    if n_prev:
        prev = pl.BlockSpec((n_prev, None, ATTN_W, tm), lambda t, pt: (0, t // tps, 0, t % tps))
        in_specs += [prev, prev]
        args += list(kv_prev)
    out_shape = [
        jax.ShapeDtypeStruct((n, ATTN_W), F32),
        jax.ShapeDtypeStruct((n_prev + 1, batch, ATTN_W, seq), F32),
        jax.ShapeDtypeStruct((n_prev + 1, batch, ATTN_W, seq), F32),
        jax.ShapeDtypeStruct((nt, tm, N_HEADS * LANES), BF16),
        jax.ShapeDtypeStruct((nt, N_HEADS * HEAD_ROWS, tm), BF16),
        jax.ShapeDtypeStruct((nt, 1, ATTN_W), F32),
        jax.ShapeDtypeStruct((n, ATTN_W), F32),
        jax.ShapeDtypeStruct((n, D_MODEL), F32),
        jax.ShapeDtypeStruct((n, D_MODEL), F32),
        jax.ShapeDtypeStruct((batch, CONV_HALO, CONV_CH), F32),
        jax.ShapeDtypeStruct((dec_batch, n_pages + 1, rows_s, PAGE_SIZE), BF16),
        jax.ShapeDtypeStruct((dec_batch, rows_s, LANES), F32),
    ]
    seq_t = pl.BlockSpec((n_prev + 1, None, ATTN_W, tm), lambda t, pt: (0, t // tps, 0, t % tps))
    out_specs = [
        row(ATTN_W), seq_t, seq_t,
        pl.BlockSpec((None, tm, N_HEADS * LANES), lambda t, pt: (t, 0, 0)),
        pl.BlockSpec((None, N_HEADS * HEAD_ROWS, tm), lambda t, pt: (t, 0, 0)),
        pl.BlockSpec((None, 1, ATTN_W), lambda t, pt: (t, 0, 0)),
        row(ATTN_W), row(D_MODEL), row(D_MODEL),
        pl.BlockSpec((None, CONV_HALO, CONV_CH), lambda t, pt: (t // tps, 0, 0)),
        pl.BlockSpec((None, n_pages + 1, rows_s, PAGE_SIZE), lambda t, pt: (t // sps, 0, 0, 0)),
        pl.BlockSpec((None, rows_s, LANES), lambda t, pt: (t // sps, 0, 0)),
    ]
    return pl.pallas_call(
        functools.partial(_proj_prompt_kernel, tiles_per_seq=tps, n_prev=n_prev, layer=layer,
                          dec_seq=dec_seq, n_pages=n_pages),
        grid_spec=pltpu.PrefetchScalarGridSpec(
            num_scalar_prefetch=1,
            grid=(nt,),
            in_specs=in_specs,
            out_specs=out_specs,
            scratch_shapes=[pltpu.VMEM((tm + HALO_PAD, CONV_CH), F32),
                            pltpu.VMEM((SUBLANES - 1, tm + HALO_PAD - SUBLANES, CONV_CH), F32),
                            pltpu.VMEM((FUSED_PAGES, N_HEADS, HEAD_DIM, PAGE_SIZE), F32),
                            pltpu.SemaphoreType.DMA((FUSED_PAGES,)),
                            pltpu.VMEM((n_pages, rows_s, PAGE_SIZE), F32),
                            pltpu.VMEM((rows_s, ATTN_W), BF16)]),
        out_shape=out_shape,
        compiler_params=pltpu.CompilerParams(dimension_semantics=("arbitrary",),
                                             vmem_limit_bytes=VMEM_LIMIT),
        name="proj_prompt",
    )(*args)


def _attn_prompt_kernel(pt_ref, q_ref, kb_ref, vt_ref, km_ref, sa_ref, g0_ref, part_ref, x_ref, wb0_ref, wout_ref,
                        gf_ref, ps_ref, pns_ref, ls_ref, vn_ref, cache_ref, o_ref, so_ref,
                        sc_scr, sel_scr, qb_scr, m_scr, acc_scr, page_buf, page_sem, sacc_scr,
                        *, final, nblk, layer, dec_seq, n_pages):
    tq = ROW_TILE
    i = pl.program_id(1)
    step = pl.program_id(0) * nblk + i
    steps_per_seq = n_pages // FUSED_PAGES
    sb, chunk = step // steps_per_seq, step % steps_per_seq
    _page_copies(cache_ref, pt_ref, page_buf, page_sem, layer, sb, chunk, start=True)
    blk_idx = lax.broadcasted_iota(jnp.int32, (nblk, tq), 0)
    valid = blk_idx < i
    lane = lax.broadcasted_iota(jnp.int32, (tq, LANES), 1)
    causal = (lax.broadcasted_iota(jnp.int32, (MOBA_BLOCK, tq), 0)
              <= lax.broadcasted_iota(jnp.int32, (MOBA_BLOCK, tq), 1))
    pair = lambda h: slice((h // 2) * LANES, (h // 2 + 1) * LANES)
    col = lambda h: slice(h * LANES, (h + 1) * LANES)
    rows = lambda h: slice(h * HEAD_ROWS, (h + 1) * HEAD_ROWS)
    bcast = lambda r: jnp.broadcast_to(r, (SUBLANES, tq))
    own_half = lambda h: jnp.where((lane // HEAD_DIM) == h % 2, q_ref[:, pair(h)], 0.0)

    for h in range(N_HEADS):
        qm = own_half(h)
        q_hi, q_lo = _split_bf16(qm)
        km_hi, km_lo = _split_bf16(km_ref[:, pair(h)])
        sc = lax.dot_general(jnp.concatenate([km_hi, km_hi, km_lo], axis=1),
                             jnp.concatenate([q_hi, q_lo, q_hi], axis=1), _NT,
                             preferred_element_type=F32)
        sc_scr[h] = jnp.where(valid, sc, -jnp.inf)
        sel_scr[h] = jnp.zeros((nblk, tq), F32)
        m_scr[h] = jnp.full((SUBLANES, tq), MAX_FLOOR, F32)
    acc_scr[...] = jnp.zeros((N_HEADS * HEAD_ROWS, tq), F32)

    def rank_body(j, carry):
        for h in range(N_HEADS):
            scm = sc_scr[h]
            sj = sc_scr[h, pl.ds(j, 1), :]
            before = (sj > scm) | ((sj == scm) & (j < blk_idx))
            sel_scr[h] = sel_scr[h] + jnp.where(before, 1.0, 0.0)
        return carry

    lax.fori_loop(0, i, rank_body, 0)
    for h in range(N_HEADS):
        keep = (valid & (sel_scr[h] < float(MOBA_TOP_K))) | (blk_idx == i)
        pieces = [jnp.where(keep, 0.0, MASK_BIAS)]
        if _block_lane(h):
            pieces.insert(0, jnp.zeros((_block_lane(h), tq), F32))
        pieces.append(jnp.zeros((LANES - _block_lane(h) - nblk, tq), F32))
        bias = jnp.concatenate(pieces, axis=0).T
        qb_scr[h] = (own_half(h) * (ATTN_SCALE * LOG2E) + bias).astype(BF16)

    def logits(j, h):
        return lax.dot_general(kb_ref[j, :, col(h)], qb_scr[h], _NT, preferred_element_type=F32)

    def softmax_step(j, h, s):
        m_old = m_scr[h, 0:1, :]
        m_new = jnp.maximum(m_old, jnp.max(s, axis=0, keepdims=True))
        alpha = jnp.exp2(m_old - m_new)
        p = jnp.exp2(s - m_new)
        m_scr[h] = bcast(m_new)
        acc_scr[rows(h), :] = alpha * acc_scr[rows(h), :] + jnp.dot(
            vt_ref[j, rows(h), :], p.astype(BF16), preferred_element_type=F32)

    def body(j, ahead):
        ahead = list(ahead)
        for h in range(N_HEADS):
            s = ahead.pop(0)
            nh = h + LOGITS_AHEAD
            ahead.append(logits(j, nh) if nh < N_HEADS else logits(j + 1, nh - N_HEADS))
            softmax_step(j, h, s)
        return tuple(ahead)

    ahead = list(lax.fori_loop(0, i, body, tuple(logits(0, h) for h in range(LOGITS_AHEAD))))
    for h in range(N_HEADS):
        s = ahead.pop(0)
        if h + LOGITS_AHEAD < N_HEADS:
            ahead.append(logits(i, h + LOGITS_AHEAD))
        softmax_step(i, h, jnp.where(causal, s, -jnp.inf))
    outs = []
    for h in range(N_HEADS):
        blk = acc_scr[rows(h), :]
        outs.append(blk[0:HEAD_DIM, :] / blk[HEAD_DIM:HEAD_DIM + 1, :])

    attn = jnp.concatenate(outs, axis=0).T
    xn = _merge_out(attn, sa_ref[...], g0_ref[...], part_ref[...], x_ref[...], wb0_ref[...], wout_ref[...])
    if final:
        xn = _rms_norm(xn, gf_ref[...])
    o_ref[...] = xn

    _page_copies(cache_ref, pt_ref, page_buf, page_sem, layer, sb, chunk, start=False)
    _sample_values_step(ps_ref, pns_ref, ls_ref, vn_ref, page_buf, sacc_scr, so_ref, chunk,
                        dec_seq=dec_seq, n_chunks=steps_per_seq)


def _attn_prompt(page_table, q, kb, vt, km, sa, g0, part, x2d, w_br_b, w_out_b, gf, p_s, l_s, v_s, cache_vt,
                 layer, batch, seq, dec_seq, final):
    tq = ROW_TILE
    nblk = seq // MOBA_BLOCK
    dec_batch, n_pages = page_table.shape
    sps = n_pages // FUSED_PAGES
    assert batch * nblk == dec_batch * sps
    rows_s = N_HEADS * dec_seq
    sstep = lambda b, i: (b * nblk + i) // sps
    row = lambda w: pl.BlockSpec((tq, w), lambda b, i, pt: (b * nblk + i, 0))
    seq_rows = pl.BlockSpec((dec_seq, ATTN_W), lambda b, i, pt: (sstep(b, i), 0))
    in_specs = [
        row(ATTN_W),
        pl.BlockSpec((nblk, MOBA_BLOCK, N_HEADS * LANES), lambda b, i, pt: (b, 0, 0)),
        pl.BlockSpec((nblk, N_HEADS * HEAD_ROWS, MOBA_BLOCK), lambda b, i, pt: (b, 0, 0)),
        pl.BlockSpec((None, nblk, ATTN_W), lambda b, i, pt: (b, 0, 0)),
        row(ATTN_W), row(D_MODEL), row(D_MODEL), row(D_MODEL),
        pl.BlockSpec((None, None, ATTN_W, D_MODEL), lambda b, i, pt: (layer, 0, 0, 0)),
        pl.BlockSpec((None, D_MODEL, D_MODEL), lambda b, i, pt: (layer, 0, 0)),
        pl.BlockSpec((1, D_MODEL), lambda b, i, pt: (0, 0)),
        pl.BlockSpec((None, FUSED_PAGES, rows_s, PAGE_SIZE),
                     lambda b, i, pt: (sstep(b, i), (b * nblk + i) % sps, 0, 0)),
        pl.BlockSpec((None, None, rows_s, PAGE_SIZE), lambda b, i, pt: (sstep(b, i), n_pages, 0, 0)),
        pl.BlockSpec((None, rows_s, LANES), lambda b, i, pt: (sstep(b, i), 0, 0)),
        seq_rows, pl.BlockSpec(memory_space=pl.ANY),
    ]
    return pl.pallas_call(
        functools.partial(_attn_prompt_kernel, final=final, nblk=nblk, layer=layer, dec_seq=dec_seq,
                          n_pages=n_pages),
        grid_spec=pltpu.PrefetchScalarGridSpec(
            num_scalar_prefetch=1,
            grid=(batch, nblk),
            in_specs=in_specs,
            out_specs=[row(D_MODEL), seq_rows],
            scratch_shapes=[pltpu.VMEM((N_HEADS, nblk, tq), F32),
                            pltpu.VMEM((N_HEADS, nblk, tq), F32),
                            pltpu.VMEM((N_HEADS, tq, LANES), BF16),
                            pltpu.VMEM((N_HEADS, SUBLANES, tq), F32),
                            pltpu.VMEM((N_HEADS * HEAD_ROWS, tq), F32),
                            pltpu.VMEM((FUSED_PAGES, N_HEADS, HEAD_DIM, PAGE_SIZE), F32),
                            pltpu.SemaphoreType.DMA((FUSED_PAGES,)),
                            pltpu.VMEM((rows_s, ATTN_W), F32)]),
        out_shape=[jax.ShapeDtypeStruct((batch * seq, D_MODEL), F32),
                   jax.ShapeDtypeStruct((dec_batch * dec_seq, ATTN_W), F32)],
        compiler_params=pltpu.CompilerParams(dimension_semantics=("arbitrary", "arbitrary"),
                                             vmem_limit_bytes=VMEM_LIMIT),
        name="attn_prompt",
    )(page_table, q, kb, vt, km, sa, g0, part, x2d, w_br_b, w_out_b, gf, p_s, p_s, l_s, v_s, cache_vt)


def _proj_sample_kernel(x_ref, g_ref, w_ref, cos_ref, sina_ref, sinb_ref, cw_ref, cb_ref, lng_ref, lnb_ref,
                        st_ref, mk_ref, mv_ref, wb1_ref, wb2_ref,
                        q_out, k_out, v_out, sa_out, g0_out, part_out, cst_out,
                        u_scr, cgate_scr, mq_scr, mgate_scr, g12_scr, c_scr, m_scr, full_scr, *, dec_seq):
    b = pl.program_id(0)
    nb = pl.num_programs(0)

    @pl.when(b == 0)
    def _():
        hb = _rms_norm(x_ref[...], g_ref[...]).astype(BF16)

        def seg(a, width):
            return jnp.dot(hb, w_ref[:, a:a + width], preferred_element_type=F32)

        cosf, sina, sinb = cos_ref[...], sina_ref[...], sinb_ref[...]
        zq = seg(_Q, ATTN_W)
        zk = seg(_K, ATTN_W)
        for c in range(ATTN_W // LANES):
            sl = slice(c * LANES, (c + 1) * LANES)
            q_out[:, sl] = _rope(zq[:, sl], cosf, sina, sinb)
            k_out[:, sl] = _rope(zk[:, sl], cosf, sina, sinb)
        v_out[...] = seg(_V, ATTN_W)
        sa_out[...] = jax.nn.silu(seg(_AG, ATTN_W))
        u_scr[...] = seg(_CV, CONV_CH) * jax.nn.sigmoid(seg(_CG, CONV_CH))
        cgate_scr[...] = jax.nn.silu(seg(_CGATE, CONV_CH))
        mq_scr[...] = seg(_MQ, MEM_W)
        mgate_scr[...] = jax.nn.silu(seg(_MG, MEM_W))
        g0_out[...] = jax.nn.sigmoid(seg(_MERGE, D_MODEL))
        g12_scr[:, 0:D_MODEL] = jax.nn.sigmoid(seg(_MERGE + D_MODEL, D_MODEL))
        g12_scr[:, D_MODEL:2 * D_MODEL] = jax.nn.sigmoid(seg(_MERGE + 2 * D_MODEL, D_MODEL))

    r0 = pl.multiple_of(b * dec_seq, dec_seq)
    rows = pl.ds(r0, dec_seq)

    full_scr[0:CONV_HALO, :] = st_ref[...]
    full_scr[CONV_HALO:CONV_HALO + dec_seq, :] = u_scr[rows, :]
    conv = jnp.broadcast_to(cb_ref[...], (dec_seq, CONV_CH))
    for t in range(CONV_K):
        conv = conv + cw_ref[t:t + 1, :] * full_scr[t:t + dec_seq, :]
    cst_out[...] = full_scr[dec_seq:dec_seq + CONV_HALO, :]
    c_scr[rows, :] = jax.nn.silu(_layer_norm(conv, lng_ref[...], lnb_ref[...])) * cgate_scr[rows, :]

    mq = mq_scr[rows, :]
    mparts = []
    for hd in range(MEM_HEADS):
        sl = slice(hd * MEM_HEAD_DIM, (hd + 1) * MEM_HEAD_DIM)
        s = lax.dot_general(mq[:, sl], mk_ref[:, hd, :], _NT, preferred_element_type=F32) * MEM_SCALE
        p = jnp.exp(s - jnp.max(s, axis=-1, keepdims=True))
        o = jnp.dot(p, mv_ref[:, hd, :], preferred_element_type=F32)
        mparts.append(o / jnp.sum(p, axis=-1, keepdims=True))
    m_scr[rows, :] = jnp.concatenate(mparts, axis=1) * mgate_scr[rows, :]

    @pl.when(b == nb - 1)
    def _():
        pc = jnp.dot(c_scr[...].astype(BF16), wb1_ref[...], preferred_element_type=F32)
        pm = jnp.dot(m_scr[...].astype(BF16), wb2_ref[...], preferred_element_type=F32)
        part_out[...] = g12_scr[:, 0:D_MODEL] * pc + g12_scr[:, D_MODEL:2 * D_MODEL] * pm


def _proj_sample(x2d, g, w_in_b, tabs, cw, cb, lng, lnb, state_conv, cache_mem_k, cache_mem_v, w_br_b,
                 layer, dec_batch, dec_seq):
    n = dec_batch * dec_seq
    full = lambda w: _const_spec((n, w))
    mem = pl.BlockSpec((None, None, MEM_LEN, MEM_HEADS, MEM_HEAD_DIM), lambda b: (layer, b, 0, 0, 0))
    wbr = lambda br: pl.BlockSpec((None, None, ATTN_W, D_MODEL), lambda b: (layer, br, 0, 0))
    in_specs = [
        full(D_MODEL), _const_spec((1, D_MODEL)),
        pl.BlockSpec((None, D_MODEL, N_IN), lambda b: (layer, 0, 0), pipeline_mode=pl.Buffered(1)),
        full(LANES), full(LANES), full(LANES),
        _const_spec((CONV_K, CONV_CH)), _const_spec((1, CONV_CH)), _const_spec((1, CONV_CH)),
        _const_spec((1, CONV_CH)),
        pl.BlockSpec((None, None, CONV_HALO, CONV_CH), lambda b: (layer, b, 0, 0)),
        mem, mem, wbr(1), wbr(2),
    ]
    out_shape = [jax.ShapeDtypeStruct((n, ATTN_W), F32)] * 4 + [
        jax.ShapeDtypeStruct((n, D_MODEL), F32), jax.ShapeDtypeStruct((n, D_MODEL), F32),
        jax.ShapeDtypeStruct((dec_batch, CONV_HALO, CONV_CH), F32)]
    out_specs = [full(ATTN_W)] * 4 + [full(D_MODEL), full(D_MODEL),
                                      pl.BlockSpec((None, CONV_HALO, CONV_CH), lambda b: (b, 0, 0))]
    scr = lambda w: pltpu.VMEM((n, w), F32)
    return pl.pallas_call(
        functools.partial(_proj_sample_kernel, dec_seq=dec_seq),
        grid=(dec_batch,),
        in_specs=in_specs,
        out_specs=out_specs,
        out_shape=out_shape,
        scratch_shapes=[scr(CONV_CH), scr(CONV_CH), scr(MEM_W), scr(MEM_W), scr(2 * D_MODEL),
                        scr(CONV_CH), scr(MEM_W),
                        pltpu.VMEM((CONV_HALO + dec_seq + 2, CONV_CH), F32)],
        compiler_params=pltpu.CompilerParams(dimension_semantics=("arbitrary",),
                                             vmem_limit_bytes=VMEM_LIMIT),
        name="proj_sample",
    )(x2d, g, w_in_b, *tabs, cw, cb, lng, lnb, state_conv, cache_mem_k, cache_mem_v, w_br_b, w_br_b)


def _page_copies(cache_ref, pt_ref, page_buf, page_sem, layer, sb, chunk, *, start):
    for r in range(FUSED_PAGES):
        page = pt_ref[sb, chunk * FUSED_PAGES + r]
        copy = pltpu.make_async_copy(cache_ref.at[layer, page], page_buf.at[r], page_sem.at[r])
        if start:
            copy.start()
        else:
            copy.wait()


def _head_rows(x, dec_seq):
    rows = N_HEADS * dec_seq
    tiled = jnp.concatenate([x] * N_HEADS, axis=0)
    row_h = lax.broadcasted_iota(jnp.int32, (rows, ATTN_W), 0) // dec_seq
    lane_h = lax.broadcasted_iota(jnp.int32, (rows, ATTN_W), 1) // HEAD_DIM
    return jnp.where(row_h == lane_h, tiled, 0.0)


def _sample_logits_pages(q_ref, page_buf, logit_scr, qb_scr, c, *, dec_seq):
    @pl.when(c == 0)
    def _():
        qb_scr[...] = (_head_rows(q_ref[...], dec_seq) * ATTN_SCALE).astype(BF16)

    qb = qb_scr[...]
    for r0 in range(0, FUSED_PAGES, PAGE_GROUP):
        kt = jnp.concatenate([page_buf[r0 + g].reshape(ATTN_W, PAGE_SIZE).astype(BF16)
                              for g in range(PAGE_GROUP)], axis=1)
        lg = jnp.dot(qb, kt, preferred_element_type=F32)
        for g in range(PAGE_GROUP):
            logit_scr[c * FUSED_PAGES + r0 + g] = lg[:, g * PAGE_SIZE:(g + 1) * PAGE_SIZE]


def _sample_logits_finish(kn_ref, logit_scr, qb_scr, p_out, l_out, c, *, dec_seq, n_pages):
    nc = n_pages // FUSED_PAGES
    rows = N_HEADS * dec_seq
    pages_per_blk = MOBA_BLOCK // PAGE_SIZE
    nblk = n_pages // pages_per_blk

    @pl.when(c == nc - 1)
    def _():
        qb = qb_scr[...]
        blk_idx = lax.broadcasted_iota(jnp.int32, (rows, nblk), 1)
        sc = jnp.zeros((rows, nblk), F32)
        for j in range(nblk):
            blk = logit_scr[j * pages_per_blk]
            for pp in range(1, pages_per_blk):
                blk = blk + logit_scr[j * pages_per_blk + pp]
            sc = jnp.where(blk_idx == j, jnp.sum(blk, axis=-1, keepdims=True), sc)
        sel = _top_k_select(sc, blk_idx >= 0, blk_idx, nblk, axis=1)
        sel_t = jnp.where(sel, 1.0, 0.0)

        kn_page = jnp.concatenate([kn_ref[...], jnp.zeros((PAGE_SIZE - dec_seq, ATTN_W), F32)], axis=0)
        ln = lax.dot_general(qb, kn_page.astype(BF16), _NT, preferred_element_type=F32)
        key_i = lax.broadcasted_iota(jnp.int32, (rows, PAGE_SIZE), 1)
        qry_i = lax.broadcasted_iota(jnp.int32, (rows, PAGE_SIZE), 0) % dec_seq
        ln = jnp.where(key_i <= qry_i, ln, -jnp.inf)

        mx = ln
        for p in range(n_pages):
            j = p // pages_per_blk
            lp = jnp.where(sel_t[:, j:j + 1] > 0.0, logit_scr[p], -jnp.inf)
            logit_scr[p] = lp
            mx = jnp.maximum(mx, lp)
        m = jnp.max(mx, axis=-1, keepdims=True)
        pn = jnp.exp(ln - m)
        p_out[n_pages] = pn.astype(BF16)
        lsum = pn
        for p in range(n_pages):
            pp = jnp.exp(logit_scr[p] - m)
            p_out[p] = pp.astype(BF16)
            lsum = lsum + pp
        l_out[...] = jnp.broadcast_to(jnp.sum(lsum, axis=-1, keepdims=True), (rows, LANES))


def _sample_values_step(p_ref, pn_ref, l_ref, vn_ref, page_buf, acc_scr, o_ref, c, *, dec_seq, n_chunks):
    @pl.when(c == 0)
    def _():
        vn_page = jnp.concatenate([vn_ref[...], jnp.zeros((PAGE_SIZE - dec_seq, ATTN_W), F32)], axis=0)
        acc_scr[...] = jnp.dot(pn_ref[...], vn_page.astype(BF16), preferred_element_type=F32)

    acc = acc_scr[...]
    for r0 in range(0, FUSED_PAGES, VALUE_GROUP):
        vt = jnp.concatenate([page_buf[r0 + g].reshape(ATTN_W, PAGE_SIZE).astype(BF16)
                              for g in range(VALUE_GROUP)], axis=1)
        pw = jnp.concatenate([p_ref[r0 + g] for g in range(VALUE_GROUP)], axis=1)
        acc = acc + lax.dot_general(pw, vt, _NT, preferred_element_type=F32)
    acc_scr[...] = acc

    @pl.when(c == n_chunks - 1)
    def _():
        res = acc / jnp.concatenate([l_ref[...]] * (ATTN_W // LANES), axis=1)
        lane_h = lax.broadcasted_iota(jnp.int32, (dec_seq, ATTN_W), 1) // HEAD_DIM
        out = jnp.zeros((dec_seq, ATTN_W), F32)
        for h in range(N_HEADS):
            out = out + jnp.where(lane_h == h, res[h * dec_seq:(h + 1) * dec_seq, :], 0.0)
        o_ref[...] = out


def _merge_sample_kernel(a_ref, sa_ref, g0_ref, part_ref, x_ref, wb0_ref, wout_ref, gf_ref, o_ref, *, final):
    xn = _merge_out(a_ref[...], sa_ref[...], g0_ref[...], part_ref[...], x_ref[...], wb0_ref[...],
                    wout_ref[...])
    if final:
        xn = _rms_norm(xn, gf_ref[...])
    o_ref[...] = xn


def _merge_sample(attn, sa, g0, part, x2d, w_br_b, w_out_b, gf, layer, final):
    n = x2d.shape[0]
    full = lambda w: _const_spec((n, w))
    return pl.pallas_call(
        functools.partial(_merge_sample_kernel, final=final),
        grid=(1,),
        in_specs=[full(ATTN_W), full(ATTN_W), full(D_MODEL), full(D_MODEL), full(D_MODEL),
                  pl.BlockSpec((None, None, ATTN_W, D_MODEL), lambda i: (layer, 0, 0, 0)),
                  pl.BlockSpec((None, D_MODEL, D_MODEL), lambda i: (layer, 0, 0)), _const_spec((1, D_MODEL))],
        out_specs=full(D_MODEL),
        out_shape=jax.ShapeDtypeStruct((n, D_MODEL), F32),
        compiler_params=pltpu.CompilerParams(dimension_semantics=("arbitrary",),
                                             vmem_limit_bytes=VMEM_LIMIT),
        name="merge_sample",
    )(attn, sa, g0, part, x2d, w_br_b, w_out_b, gf)


def kernel(x_prompt, x_sample, cache_k, cache_v, cache_mem_k, cache_mem_v, state_conv, page_table, mem_prompt,
           g_norm, w_in, conv_w, conv_b, ln_g, ln_b, w_mem_k, w_mem_v, w_branch, w_out, g_final):
    batch, seq, _ = x_prompt.shape
    dec_batch, dec_seq, _ = x_sample.shape
    depth = w_in.shape[0]
    past_len = page_table.shape[1] * PAGE_SIZE
    assert seq % ROW_TILE == 0 and ROW_TILE == MOBA_BLOCK
    assert dec_batch * dec_seq == ROW_TILE and dec_seq == 8 and past_len % MOBA_BLOCK == 0
    assert page_table.shape[1] % FUSED_PAGES == 0

    w_in_b = w_in.astype(BF16)
    w_br_b = w_branch.astype(BF16)
    w_out_b = w_out.astype(BF16)
    gf = g_final.reshape(1, D_MODEL)

    mem_k_p, mem_v_p = _mem_proj(mem_prompt.reshape(batch * MEM_LEN, D_MODEL), w_mem_k, w_mem_v)

    tabs_p = _rope_tables(jnp.arange(seq, dtype=jnp.int32))
    tabs_s = tuple(jnp.tile(t, (dec_batch, 1))
                   for t in _rope_tables(past_len + jnp.arange(dec_seq, dtype=jnp.int32)))

    cache_kt = cache_k.transpose(0, 1, 3, 4, 2)
    cache_vt = cache_v.transpose(0, 1, 3, 4, 2)

    xp = x_prompt.reshape(batch * seq, D_MODEL)
    xs = x_sample.reshape(dec_batch * dec_seq, D_MODEL)
    kv_p = None
    cp_l, ks_l, vs_l, cs_l = [], [], [], []
    for l in range(depth):
        g = g_norm[l].reshape(1, D_MODEL)
        cw, cb = conv_w[l], conv_b[l].reshape(1, CONV_CH)
        lng, lnb = ln_g[l].reshape(1, CONV_CH), ln_b[l].reshape(1, CONV_CH)
        final = l == depth - 1

        qs, k_s, v_s, sas, g0s, parts, csts = _proj_sample(
            xs, g, w_in_b, tabs_s, cw, cb, lng, lnb, state_conv, cache_mem_k, cache_mem_v, w_br_b,
            l, dec_batch, dec_seq)
        q, kt, vt, kb, vtb, km, sa, g0, part, cst, p_s, l_s = _proj_prompt(
            page_table, xp, g, w_in_b, tabs_p, cw, cb, lng, lnb, mem_k_p, mem_v_p, w_br_b, qs, k_s, cache_kt,
            kv_p, l, batch, seq, dec_seq)
        xp, attn_s = _attn_prompt(
            page_table, q, kb, vtb, km.reshape(batch, seq // MOBA_BLOCK, ATTN_W), sa, g0, part, xp,
            w_br_b, w_out_b, gf, p_s, l_s, v_s, cache_vt, l, batch, seq, dec_seq, final)
        kv_p = (kt, vt)
        cp_l.append(cst)
        xs = _merge_sample(attn_s, sas, g0s, parts, xs, w_br_b, w_out_b, gf, l, final)
        ks_l.append(k_s); vs_l.append(v_s); cs_l.append(csts)

    y_prompt = xp.reshape(batch, seq, D_MODEL)
    y_sample = xs.reshape(dec_batch, dec_seq, D_MODEL)
    head_p = lambda t: t.reshape(depth, batch, N_HEADS, HEAD_DIM, seq).transpose(0, 1, 4, 2, 3)
    head_s = lambda ts: jnp.stack(ts).reshape(depth, dec_batch, dec_seq, N_HEADS, HEAD_DIM)
    mem_shape = (depth, batch, MEM_LEN, MEM_HEADS, MEM_HEAD_DIM)
    return (y_prompt, y_sample, head_p(kv_p[0]), head_p(kv_p[1]), jnp.stack(cp_l),
            mem_k_p.reshape(mem_shape), mem_v_p.reshape(mem_shape),
            head_s(ks_l), head_s(vs_l), jnp.stack(cs_l))
```

```python
import functools

import jax
import jax.numpy as jnp
from jax import lax
from jax.experimental import pallas as pl
from jax.experimental.pallas import tpu as pltpu

F32 = jnp.float32
BF16 = jnp.bfloat16

D_MODEL = 1024
N_HEADS = 8
HEAD_DIM = 64
ATTN_W = N_HEADS * HEAD_DIM
ROT_DIM = HEAD_DIM // 4
ROPE_THETA = 500000.0
MOBA_BLOCK = 256
MOBA_TOP_K = 3
CONV_CH = 512
CONV_K = 31
CONV_HALO = CONV_K - 1
MEM_LEN = 256
MEM_HEADS = 4
MEM_HEAD_DIM = 128
MEM_W = MEM_HEADS * MEM_HEAD_DIM
PAGE_SIZE = 128
RMS_EPS = 1e-6
LN_EPS = 1e-5
ATTN_SCALE = HEAD_DIM ** -0.5
MEM_SCALE = MEM_HEAD_DIM ** -0.5

LANES = 128
SUBLANES = 8
LOG2E = 1.4426950408889634
MASK_BIAS = -1e30
MAX_FLOOR = -1e29
HEAD_ROWS = HEAD_DIM + 16
ROW_TILE = 256
LOGITS_AHEAD = 4
FUSED_PAGES = 32
PAGE_GROUP = 2
VALUE_GROUP = 8
SEQS_PER_STEP = 4
HALO_PAD = 32
VMEM_LIMIT = 56 * 1024 * 1024

_Q, _K, _V, _AG, _CV, _CG, _CGATE, _MQ, _MG, _MERGE = (
    0, 512, 1024, 1536, 2048, 2560, 3072, 3584, 4096, 4608)
N_IN = _MERGE + 3 * D_MODEL

_NT = (((1,), (1,)), ((), ()))


def _rms_norm(x, g):
    return x * lax.rsqrt(jnp.mean(x * x, axis=-1, keepdims=True) + RMS_EPS) * g


def _layer_norm(x, g, b):
    mu = jnp.mean(x, axis=-1, keepdims=True)
    xc = x - mu
    var = jnp.mean(xc * xc, axis=-1, keepdims=True)
    return xc * lax.rsqrt(var + LN_EPS) * g + b


def _rope(xc, cosf, sina, sinb):
    return (xc * cosf + pltpu.roll(xc, LANES - ROT_DIM // 2, 1) * sina
            + pltpu.roll(xc, ROT_DIM // 2, 1) * sinb)


def _rope_tables(pos):
    half = ROT_DIM // 2
    inv = ROPE_THETA ** (-jnp.arange(0, ROT_DIM, 2, dtype=F32) / ROT_DIM)
    ang = pos.astype(F32)[:, None] * inv[None, :]
    cos, sin = jnp.cos(ang), jnp.sin(ang)
    n = pos.shape[0]
    zeros_h = jnp.zeros((n, half), F32)
    rest0 = jnp.zeros((n, HEAD_DIM - ROT_DIM), F32)
    cosf = jnp.concatenate([cos, cos, jnp.ones((n, HEAD_DIM - ROT_DIM), F32)], axis=1)
    sina = jnp.concatenate([-sin, zeros_h, rest0], axis=1)
    sinb = jnp.concatenate([zeros_h, sin, rest0], axis=1)
    rep = LANES // HEAD_DIM
    return tuple(jnp.tile(t, (1, rep)) for t in (cosf, sina, sinb))


def _top_k_select(sc, valid, idx, n, axis=0):
    scm = jnp.where(valid, sc, -jnp.inf)
    rank = jnp.zeros(sc.shape, F32)
    for j in range(n):
        sj = scm[j:j + 1, :] if axis == 0 else scm[:, j:j + 1]
        beats = (sj > scm) | ((sj == scm) & (j < idx))
        rank = rank + jnp.where(beats, 1.0, 0.0)
    return valid & (rank < float(MOBA_TOP_K))


def _block_lane(h):
    return HEAD_DIM if h % 2 == 0 else 0


def _split_bf16(x):
    hi = x.astype(BF16)
    return hi, (x - hi.astype(F32)).astype(BF16)


def _merge_out(attn, sa, g0, part, x, wb0, wout):
    a = (attn * sa).astype(BF16)
    pa = jnp.dot(a, wb0, preferred_element_type=F32)
    mix = (g0 * pa + part).astype(BF16)
    return x + jnp.dot(mix, wout, preferred_element_type=F32)


def _mem_proj_kernel(mem_ref, wk_ref, wv_ref, mk_out, mv_out):
    mb = mem_ref[...].astype(BF16)
    mk_out[...] = jnp.dot(mb, wk_ref[...].astype(BF16), preferred_element_type=F32)
    mv_out[...] = jnp.dot(mb, wv_ref[...].astype(BF16), preferred_element_type=F32)


def _mem_proj(mem2d, w_mem_k, w_mem_v):
    depth = w_mem_k.shape[0]
    rows = mem2d.shape[0]
    w_spec = pl.BlockSpec((None, D_MODEL, MEM_W), lambda l: (l, 0, 0))
    o_spec = pl.BlockSpec((None, rows, MEM_W), lambda l: (l, 0, 0))
    return pl.pallas_call(
        _mem_proj_kernel,
        grid=(depth,),
        in_specs=[pl.BlockSpec((rows, D_MODEL), lambda l: (0, 0)), w_spec, w_spec],
        out_specs=[o_spec, o_spec],
        out_shape=[jax.ShapeDtypeStruct((depth, rows, MEM_W), F32)] * 2,
        compiler_params=pltpu.CompilerParams(dimension_semantics=("arbitrary",),
                                             vmem_limit_bytes=VMEM_LIMIT),
        name="mem_proj",
    )(mem2d, w_mem_k, w_mem_v)


def _proj_prompt_kernel(pt_ref, x_ref, g_ref, w_ref, cos_ref, sina_ref, sinb_ref, cw_ref, cb_ref, lng_ref, lnb_ref,
                        mk_ref, mv_ref, wb1_ref, wb2_ref, qs_ref, kn_ref, cache_ref, *refs,
                        tiles_per_seq, n_prev, layer, dec_seq, n_pages):
    if n_prev:
        ktp_ref, vtp_ref = refs[:2]
        refs = refs[2:]
    (q_out, kt_out, vt_out, kb_out, vtb_out, km_out, sa_out, g0_out, part_out, cst_out, p_out, l_out,
     ubuf, urot, page_buf, page_sem, logit_scr, qb_scr) = refs
    tm = ROW_TILE
    step = pl.program_id(0)
    tin = step % tiles_per_seq
    steps_per_seq = n_pages // FUSED_PAGES
    sb, chunk = step // steps_per_seq, step % steps_per_seq
    _page_copies(cache_ref, pt_ref, page_buf, page_sem, layer, sb, chunk, start=True)
    if n_prev:
        kt_out[0:n_prev] = ktp_ref[...]
        vt_out[0:n_prev] = vtp_ref[...]
    hb = _rms_norm(x_ref[...], g_ref[...]).astype(BF16)

    def seg(a, width):
        return jnp.dot(hb, w_ref[:, a:a + width], preferred_element_type=F32)

    cosf, sina, sinb = cos_ref[...], sina_ref[...], sinb_ref[...]
    zq = seg(_Q, ATTN_W)
    zk = seg(_K, ATTN_W)
    for c in range(ATTN_W // LANES):
        sl = slice(c * LANES, (c + 1) * LANES)
        q_out[:, sl] = _rope(zq[:, sl], cosf, sina, sinb)
        kr = _rope(zk[:, sl], cosf, sina, sinb)
        kt_out[n_prev, sl, :] = kr.T
        lane = lax.broadcasted_iota(jnp.int32, (tm, LANES), 1)
        for hh in range(2):
            onehot = jnp.where(lane == _block_lane(2 * c + hh) + tin, 1.0, 0.0)
            kb_out[:, (2 * c + hh) * LANES:(2 * c + hh + 1) * LANES] = jnp.where(
                lane // HEAD_DIM == hh, kr, onehot).astype(BF16)
        km_out[:, sl] = jnp.mean(kr, axis=0, keepdims=True)
    zvt = seg(_V, ATTN_W).T
    vt_out[n_prev] = zvt
    for h in range(N_HEADS):
        vtb_out[h * HEAD_ROWS:h * HEAD_ROWS + HEAD_DIM, :] = zvt[h * HEAD_DIM:(h + 1) * HEAD_DIM, :].astype(BF16)
        vtb_out[h * HEAD_ROWS + HEAD_DIM:(h + 1) * HEAD_ROWS, :] = jnp.ones((HEAD_ROWS - HEAD_DIM, tm), BF16)
    sa_out[...] = jax.nn.silu(seg(_AG, ATTN_W))

    u = seg(_CV, CONV_CH) * jax.nn.sigmoid(seg(_CG, CONV_CH))

    @pl.when(tin == 0)
    def _():
        ubuf[0:HALO_PAD, :] = jnp.zeros((HALO_PAD, CONV_CH), F32)

    @pl.when(tin != 0)
    def _():
        ubuf[HALO_PAD - CONV_HALO:HALO_PAD, :] = ubuf[tm + HALO_PAD - CONV_HALO:tm + HALO_PAD, :]

    ubuf[HALO_PAD:HALO_PAD + tm, :] = u
    nrot = tm + HALO_PAD - SUBLANES
    for r in range(1, SUBLANES):
        urot[r - 1] = ubuf[r:r + nrot, :]
    cst_out[...] = ubuf[tm + HALO_PAD - CONV_HALO:tm + HALO_PAD, :]
    conv = jnp.broadcast_to(cb_ref[...], (tm, CONV_CH))
    for t in range(CONV_K):
        a, r = divmod(HALO_PAD - CONV_HALO + t, SUBLANES)
        rows = slice(a * SUBLANES, a * SUBLANES + tm)
        conv = conv + cw_ref[t:t + 1, :] * (ubuf[rows, :] if r == 0 else urot[r - 1, rows, :])
    cbr = jax.nn.silu(_layer_norm(conv, lng_ref[...], lnb_ref[...])) * jax.nn.silu(seg(_CGATE, CONV_CH))

    zmq = seg(_MQ, MEM_W)
    mparts = []
    for hd in range(MEM_HEADS):
        sl = slice(hd * MEM_HEAD_DIM, (hd + 1) * MEM_HEAD_DIM)
        s = lax.dot_general(zmq[:, sl].astype(BF16), mk_ref[:, sl].astype(BF16), _NT,
                            preferred_element_type=F32) * MEM_SCALE
        p = jnp.exp(s - jnp.max(s, axis=-1, keepdims=True))
        o = jnp.dot(p.astype(BF16), mv_ref[:, sl].astype(BF16), preferred_element_type=F32)
        mparts.append(o / jnp.sum(p, axis=-1, keepdims=True))
    mbr = jnp.concatenate(mparts, axis=1) * jax.nn.silu(seg(_MG, MEM_W))

    pc = jnp.dot(cbr.astype(BF16), wb1_ref[...], preferred_element_type=F32)
    pm = jnp.dot(mbr.astype(BF16), wb2_ref[...], preferred_element_type=F32)
    g0_out[...] = jax.nn.sigmoid(seg(_MERGE, D_MODEL))
    part_out[...] = (jax.nn.sigmoid(seg(_MERGE + D_MODEL, D_MODEL)) * pc
                     + jax.nn.sigmoid(seg(_MERGE + 2 * D_MODEL, D_MODEL)) * pm)

    _page_copies(cache_ref, pt_ref, page_buf, page_sem, layer, sb, chunk, start=False)
    _sample_logits_pages(qs_ref, page_buf, logit_scr, qb_scr, chunk, dec_seq=dec_seq)
    _sample_logits_finish(kn_ref, logit_scr, qb_scr, p_out, l_out, chunk, dec_seq=dec_seq, n_pages=n_pages)


def _const_spec(shape, ngrid=1):
    zeros = (0,) * len(shape)
    if ngrid == 1:
        return pl.BlockSpec(shape, lambda i: zeros)
    return pl.BlockSpec(shape, lambda i, j: zeros)


def _proj_prompt(page_table, x2d, g, w_in_b, tabs, cw, cb, lng, lnb, mk, mv, w_br_b, q_s, k_s, cache_kt,
                 kv_prev, layer, batch, seq, dec_seq):
    tm = ROW_TILE
    n = batch * seq
    tps = seq // tm
    nt = n // tm
    dec_batch, n_pages = page_table.shape
    sps = n_pages // FUSED_PAGES
    assert nt == dec_batch * sps
    rows_s = N_HEADS * dec_seq
    n_prev = 0 if kv_prev is None else kv_prev[0].shape[0]
    const = lambda shape: pl.BlockSpec(shape, lambda t, pt: (0,) * len(shape))
    row = lambda w: pl.BlockSpec((tm, w), lambda t, pt: (t, 0))
    tab = pl.BlockSpec((tm, LANES), lambda t, pt: (t % tps, 0))
    mem = pl.BlockSpec((None, MEM_LEN, MEM_W), lambda t, pt: (layer, t // tps, 0))
    wbr = lambda br: pl.BlockSpec((None, None, ATTN_W, D_MODEL), lambda t, pt: (layer, br, 0, 0))
    seq_rows = pl.BlockSpec((dec_seq, ATTN_W), lambda t, pt: (t // sps, 0))
    in_specs = [
        row(D_MODEL), const((1, D_MODEL)),
        pl.BlockSpec((None, D_MODEL, N_IN), lambda t, pt: (layer, 0, 0), pipeline_mode=pl.Buffered(1)),
        tab, tab, tab,
        const((CONV_K, CONV_CH)), const((1, CONV_CH)), const((1, CONV_CH)), const((1, CONV_CH)),
        mem, mem, wbr(1), wbr(2),
        seq_rows, seq_rows, pl.BlockSpec(memory_space=pl.ANY),
    ]
    args = [page_table, x2d, g, w_in_b, *tabs, cw, cb, lng, lnb, mk, mv, w_br_b, w_br_b, q_s, k_s, cache_kt]
    if n_prev:
        prev = pl.BlockSpec((n_prev, None, ATTN_W, tm), lambda t, pt: (0, t // tps, 0, t % tps))
        in_specs += [prev, prev]
        args += list(kv_prev)
    out_shape = [
        jax.ShapeDtypeStruct((n, ATTN_W), F32),
        jax.ShapeDtypeStruct((n_prev + 1, batch, ATTN_W, seq), F32),
        jax.ShapeDtypeStruct((n_prev + 1, batch, ATTN_W, seq), F32),
        jax.ShapeDtypeStruct((nt, tm, N_HEADS * LANES), BF16),
        jax.ShapeDtypeStruct((nt, N_HEADS * HEAD_ROWS, tm), BF16),
        jax.ShapeDtypeStruct((nt, 1, ATTN_W), F32),
        jax.ShapeDtypeStruct((n, ATTN_W), F32),
        jax.ShapeDtypeStruct((n, D_MODEL), F32),
        jax.ShapeDtypeStruct((n, D_MODEL), F32),
        jax.ShapeDtypeStruct((batch, CONV_HALO, CONV_CH), F32),
        jax.ShapeDtypeStruct((dec_batch, n_pages + 1, rows_s, PAGE_SIZE), BF16),
        jax.ShapeDtypeStruct((dec_batch, rows_s, LANES), F32),
    ]
    seq_t = pl.BlockSpec((n_prev + 1, None, ATTN_W, tm), lambda t, pt: (0, t // tps, 0, t % tps))
    out_specs = [
        row(ATTN_W), seq_t, seq_t,
        pl.BlockSpec((None, tm, N_HEADS * LANES), lambda t, pt: (t, 0, 0)),
        pl.BlockSpec((None, N_HEADS * HEAD_ROWS, tm), lambda t, pt: (t, 0, 0)),
        pl.BlockSpec((None, 1, ATTN_W), lambda t, pt: (t, 0, 0)),
        row(ATTN_W), row(D_MODEL), row(D_MODEL),
        pl.BlockSpec((None, CONV_HALO, CONV_CH), lambda t, pt: (t // tps, 0, 0)),
        pl.BlockSpec((None, n_pages + 1, rows_s, PAGE_SIZE), lambda t, pt: (t // sps, 0, 0, 0)),
        pl.BlockSpec((None, rows_s, LANES), lambda t, pt: (t // sps, 0, 0)),
    ]
    return pl.pallas_call(
        functools.partial(_proj_prompt_kernel, tiles_per_seq=tps, n_prev=n_prev, layer=layer,
                          dec_seq=dec_seq, n_pages=n_pages),
        grid_spec=pltpu.PrefetchScalarGridSpec(
            num_scalar_prefetch=1,
            grid=(nt,),
            in_specs=in_specs,
            out_specs=out_specs,
            scratch_shapes=[pltpu.VMEM((tm + HALO_PAD, CONV_CH), F32),
                            pltpu.VMEM((SUBLANES - 1, tm + HALO_PAD - SUBLANES, CONV_CH), F32),
                            pltpu.VMEM((FUSED_PAGES, N_HEADS, HEAD_DIM, PAGE_SIZE), F32),
                            pltpu.SemaphoreType.DMA((FUSED_PAGES,)),
                            pltpu.VMEM((n_pages, rows_s, PAGE_SIZE), F32),
                            pltpu.VMEM((rows_s, ATTN_W), BF16)]),
        out_shape=out_shape,
        compiler_params=pltpu.CompilerParams(dimension_semantics=("arbitrary",),
                                             vmem_limit_bytes=VMEM_LIMIT),
        name="proj_prompt",
    )(*args)


def _attn_prompt_kernel(pt_ref, q_ref, kb_ref, vt_ref, km_ref, sa_ref, g0_ref, part_ref, x_ref, wb0_ref, wout_ref,
                        gf_ref, ps_ref, pns_ref, ls_ref, vn_ref, cache_ref, o_ref, so_ref,
                        sc_scr, sel_scr, qb_scr, m_scr, acc_scr, page_buf, page_sem, sacc_scr,
                        *, final, nblk, layer, dec_seq, n_pages):
    tq = ROW_TILE
    i = pl.program_id(1)
    step = pl.program_id(0) * nblk + i
    steps_per_seq = n_pages // FUSED_PAGES
    sb, chunk = step // steps_per_seq, step % steps_per_seq
    _page_copies(cache_ref, pt_ref, page_buf, page_sem, layer, sb, chunk, start=True)
    blk_idx = lax.broadcasted_iota(jnp.int32, (nblk, tq), 0)
    valid = blk_idx < i
    lane = lax.broadcasted_iota(jnp.int32, (tq, LANES), 1)
    causal = (lax.broadcasted_iota(jnp.int32, (MOBA_BLOCK, tq), 0)
              <= lax.broadcasted_iota(jnp.int32, (MOBA_BLOCK, tq), 1))
    pair = lambda h: slice((h // 2) * LANES, (h // 2 + 1) * LANES)
    col = lambda h: slice(h * LANES, (h + 1) * LANES)
    rows = lambda h: slice(h * HEAD_ROWS, (h + 1) * HEAD_ROWS)
    bcast = lambda r: jnp.broadcast_to(r, (SUBLANES, tq))
    own_half = lambda h: jnp.where((lane // HEAD_DIM) == h % 2, q_ref[:, pair(h)], 0.0)

    for h in range(N_HEADS):
        qm = own_half(h)
        q_hi, q_lo = _split_bf16(qm)
        km_hi, km_lo = _split_bf16(km_ref[:, pair(h)])
        sc = lax.dot_general(jnp.concatenate([km_hi, km_hi, km_lo], axis=1),
                             jnp.concatenate([q_hi, q_lo, q_hi], axis=1), _NT,
                             preferred_element_type=F32)
        sc_scr[h] = jnp.where(valid, sc, -jnp.inf)
        sel_scr[h] = jnp.zeros((nblk, tq), F32)
        m_scr[h] = jnp.full((SUBLANES, tq), MAX_FLOOR, F32)
    acc_scr[...] = jnp.zeros((N_HEADS * HEAD_ROWS, tq), F32)

    def rank_body(j, carry):
        for h in range(N_HEADS):
            scm = sc_scr[h]
            sj = sc_scr[h, pl.ds(j, 1), :]
            before = (sj > scm) | ((sj == scm) & (j < blk_idx))
            sel_scr[h] = sel_scr[h] + jnp.where(before, 1.0, 0.0)
        return carry

    lax.fori_loop(0, i, rank_body, 0)
    for h in range(N_HEADS):
        keep = (valid & (sel_scr[h] < float(MOBA_TOP_K))) | (blk_idx == i)
        pieces = [jnp.where(keep, 0.0, MASK_BIAS)]
        if _block_lane(h):
            pieces.insert(0, jnp.zeros((_block_lane(h), tq), F32))
        pieces.append(jnp.zeros((LANES - _block_lane(h) - nblk, tq), F32))
        bias = jnp.concatenate(pieces, axis=0).T
        qb_scr[h] = (own_half(h) * (ATTN_SCALE * LOG2E) + bias).astype(BF16)

    def logits(j, h):
        return lax.dot_general(kb_ref[j, :, col(h)], qb_scr[h], _NT, preferred_element_type=F32)

    def softmax_step(j, h, s):
        m_old = m_scr[h, 0:1, :]
        m_new = jnp.maximum(m_old, jnp.max(s, axis=0, keepdims=True))
        alpha = jnp.exp2(m_old - m_new)
        p = jnp.exp2(s - m_new)
        m_scr[h] = bcast(m_new)
        acc_scr[rows(h), :] = alpha * acc_scr[rows(h), :] + jnp.dot(
            vt_ref[j, rows(h), :], p.astype(BF16), preferred_element_type=F32)

    def body(j, ahead):
        ahead = list(ahead)
        for h in range(N_HEADS):
            s = ahead.pop(0)
            nh = h + LOGITS_AHEAD
            ahead.append(logits(j, nh) if nh < N_HEADS else logits(j + 1, nh - N_HEADS))
            softmax_step(j, h, s)
        return tuple(ahead)

    ahead = list(lax.fori_loop(0, i, body, tuple(logits(0, h) for h in range(LOGITS_AHEAD))))
    for h in range(N_HEADS):
        s = ahead.pop(0)
        if h + LOGITS_AHEAD < N_HEADS:
            ahead.append(logits(i, h + LOGITS_AHEAD))
        softmax_step(i, h, jnp.where(causal, s, -jnp.inf))
    outs = []
    for h in range(N_HEADS):
        blk = acc_scr[rows(h), :]
        outs.append(blk[0:HEAD_DIM, :] / blk[HEAD_DIM:HEAD_DIM + 1, :])

    attn = jnp.concatenate(outs, axis=0).T
    xn = _merge_out(attn, sa_ref[...], g0_ref[...], part_ref[...], x_ref[...], wb0_ref[...], wout_ref[...])
    if final:
        xn = _rms_norm(xn, gf_ref[...])
    o_ref[...] = xn

    _page_copies(cache_ref, pt_ref, page_buf, page_sem, layer, sb, chunk, start=False)
    _sample_values_step(ps_ref, pns_ref, ls_ref, vn_ref, page_buf, sacc_scr, so_ref, chunk,
                        dec_seq=dec_seq, n_chunks=steps_per_seq)


def _attn_prompt(page_table, q, kb, vt, km, sa, g0, part, x2d, w_br_b, w_out_b, gf, p_s, l_s, v_s, cache_vt,
                 layer, batch, seq, dec_seq, final):
    tq = ROW_TILE
    nblk = seq // MOBA_BLOCK
    dec_batch, n_pages = page_table.shape
    sps = n_pages // FUSED_PAGES
    assert batch * nblk == dec_batch * sps
    rows_s = N_HEADS * dec_seq
    sstep = lambda b, i: (b * nblk + i) // sps
    row = lambda w: pl.BlockSpec((tq, w), lambda b, i, pt: (b * nblk + i, 0))
    seq_rows = pl.BlockSpec((dec_seq, ATTN_W), lambda b, i, pt: (sstep(b, i), 0))
    in_specs = [
        row(ATTN_W),
        pl.BlockSpec((nblk, MOBA_BLOCK, N_HEADS * LANES), lambda b, i, pt: (b, 0, 0)),
        pl.BlockSpec((nblk, N_HEADS * HEAD_ROWS, MOBA_BLOCK), lambda b, i, pt: (b, 0, 0)),
        pl.BlockSpec((None, nblk, ATTN_W), lambda b, i, pt: (b, 0, 0)),
        row(ATTN_W), row(D_MODEL), row(D_MODEL), row(D_MODEL),
        pl.BlockSpec((None, None, ATTN_W, D_MODEL), lambda b, i, pt: (layer, 0, 0, 0)),
        pl.BlockSpec((None, D_MODEL, D_MODEL), lambda b, i, pt: (layer, 0, 0)),
        pl.BlockSpec((1, D_MODEL), lambda b, i, pt: (0, 0)),
        pl.BlockSpec((None, FUSED_PAGES, rows_s, PAGE_SIZE),
                     lambda b, i, pt: (sstep(b, i), (b * nblk + i) % sps, 0, 0)),
        pl.BlockSpec((None, None, rows_s, PAGE_SIZE), lambda b, i, pt: (sstep(b, i), n_pages, 0, 0)),
        pl.BlockSpec((None, rows_s, LANES), lambda b, i, pt: (sstep(b, i), 0, 0)),
        seq_rows, pl.BlockSpec(memory_space=pl.ANY),
    ]
    return pl.pallas_call(
        functools.partial(_attn_prompt_kernel, final=final, nblk=nblk, layer=layer, dec_seq=dec_seq,
                          n_pages=n_pages),
        grid_spec=pltpu.PrefetchScalarGridSpec(
            num_scalar_prefetch=1,
            grid=(batch, nblk),
            in_specs=in_specs,
            out_specs=[row(D_MODEL), seq_rows],
            scratch_shapes=[pltpu.VMEM((N_HEADS, nblk, tq), F32),
                            pltpu.VMEM((N_HEADS, nblk, tq), F32),
                            pltpu.VMEM((N_HEADS, tq, LANES), BF16),
                            pltpu.VMEM((N_HEADS, SUBLANES, tq), F32),
                            pltpu.VMEM((N_HEADS * HEAD_ROWS, tq), F32),
                            pltpu.VMEM((FUSED_PAGES, N_HEADS, HEAD_DIM, PAGE_SIZE), F32),
                            pltpu.SemaphoreType.DMA((FUSED_PAGES,)),
                            pltpu.VMEM((rows_s, ATTN_W), F32)]),
        out_shape=[jax.ShapeDtypeStruct((batch * seq, D_MODEL), F32),
                   jax.ShapeDtypeStruct((dec_batch * dec_seq, ATTN_W), F32)],
        compiler_params=pltpu.CompilerParams(dimension_semantics=("arbitrary", "arbitrary"),
                                             vmem_limit_bytes=VMEM_LIMIT),
        name="attn_prompt",
    )(page_table, q, kb, vt, km, sa, g0, part, x2d, w_br_b, w_out_b, gf, p_s, p_s, l_s, v_s, cache_vt)


def _proj_sample_kernel(x_ref, g_ref, w_ref, cos_ref, sina_ref, sinb_ref, cw_ref, cb_ref, lng_ref, lnb_ref,
                        st_ref, mk_ref, mv_ref, wb1_ref, wb2_ref,
                        q_out, k_out, v_out, sa_out, g0_out, part_out, cst_out,
                        u_scr, cgate_scr, mq_scr, mgate_scr, g12_scr, c_scr, m_scr, full_scr, *, dec_seq):
    b = pl.program_id(0)
    nb = pl.num_programs(0)

    @pl.when(b == 0)
    def _():
        hb = _rms_norm(x_ref[...], g_ref[...]).astype(BF16)

        def seg(a, width):
            return jnp.dot(hb, w_ref[:, a:a + width], preferred_element_type=F32)

        cosf, sina, sinb = cos_ref[...], sina_ref[...], sinb_ref[...]
        zq = seg(_Q, ATTN_W)
        zk = seg(_K, ATTN_W)
        for c in range(ATTN_W // LANES):
            sl = slice(c * LANES, (c + 1) * LANES)
            q_out[:, sl] = _rope(zq[:, sl], cosf, sina, sinb)
            k_out[:, sl] = _rope(zk[:, sl], cosf, sina, sinb)
        v_out[...] = seg(_V, ATTN_W)
        sa_out[...] = jax.nn.silu(seg(_AG, ATTN_W))
        u_scr[...] = seg(_CV, CONV_CH) * jax.nn.sigmoid(seg(_CG, CONV_CH))
        cgate_scr[...] = jax.nn.silu(seg(_CGATE, CONV_CH))
        mq_scr[...] = seg(_MQ, MEM_W)
        mgate_scr[...] = jax.nn.silu(seg(_MG, MEM_W))
        g0_out[...] = jax.nn.sigmoid(seg(_MERGE, D_MODEL))
        g12_scr[:, 0:D_MODEL] = jax.nn.sigmoid(seg(_MERGE + D_MODEL, D_MODEL))
        g12_scr[:, D_MODEL:2 * D_MODEL] = jax.nn.sigmoid(seg(_MERGE + 2 * D_MODEL, D_MODEL))

    for sq in range(SEQS_PER_STEP):
        r0 = pl.multiple_of((b * SEQS_PER_STEP + sq) * dec_seq, dec_seq)
        rows = pl.ds(r0, dec_seq)

        full_scr[sq, 0:CONV_HALO, :] = st_ref[sq]
        full_scr[sq, CONV_HALO:CONV_HALO + dec_seq, :] = u_scr[rows, :]
        conv = jnp.broadcast_to(cb_ref[...], (dec_seq, CONV_CH))
        for t in range(CONV_K):
            conv = conv + cw_ref[t:t + 1, :] * full_scr[sq, t:t + dec_seq, :]
        cst_out[sq] = full_scr[sq, dec_seq:dec_seq + CONV_HALO, :]
        c_scr[rows, :] = jax.nn.silu(_layer_norm(conv, lng_ref[...], lnb_ref[...])) * cgate_scr[rows, :]

        mq = mq_scr[rows, :]
        mparts = []
        for hd in range(MEM_HEADS):
            sl = slice(hd * MEM_HEAD_DIM, (hd + 1) * MEM_HEAD_DIM)
            s = lax.dot_general(mq[:, sl], mk_ref[sq, :, hd, :], _NT, preferred_element_type=F32) * MEM_SCALE
            p = jnp.exp(s - jnp.max(s, axis=-1, keepdims=True))
            o = jnp.dot(p, mv_ref[sq, :, hd, :], preferred_element_type=F32)
            mparts.append(o / jnp.sum(p, axis=-1, keepdims=True))
        m_scr[rows, :] = jnp.concatenate(mparts, axis=1) * mgate_scr[rows, :]

    @pl.when(b == nb - 1)
    def _():
        pc = jnp.dot(c_scr[...].astype(BF16), wb1_ref[...], preferred_element_type=F32)
        pm = jnp.dot(m_scr[...].astype(BF16), wb2_ref[...], preferred_element_type=F32)
        part_out[...] = g12_scr[:, 0:D_MODEL] * pc + g12_scr[:, D_MODEL:2 * D_MODEL] * pm


def _proj_sample(x2d, g, w_in_b, tabs, cw, cb, lng, lnb, state_conv, cache_mem_k, cache_mem_v, w_br_b,
                 layer, dec_batch, dec_seq):
    n = dec_batch * dec_seq
    full = lambda w: _const_spec((n, w))
    assert dec_batch % SEQS_PER_STEP == 0
    mem = pl.BlockSpec((None, SEQS_PER_STEP, MEM_LEN, MEM_HEADS, MEM_HEAD_DIM), lambda b: (layer, b, 0, 0, 0))
    wbr = lambda br: pl.BlockSpec((None, None, ATTN_W, D_MODEL), lambda b: (layer, br, 0, 0))
    in_specs = [
        full(D_MODEL), _const_spec((1, D_MODEL)),
        pl.BlockSpec((None, D_MODEL, N_IN), lambda b: (layer, 0, 0), pipeline_mode=pl.Buffered(1)),
        full(LANES), full(LANES), full(LANES),
        _const_spec((CONV_K, CONV_CH)), _const_spec((1, CONV_CH)), _const_spec((1, CONV_CH)),
        _const_spec((1, CONV_CH)),
        pl.BlockSpec((None, SEQS_PER_STEP, CONV_HALO, CONV_CH), lambda b: (layer, b, 0, 0)),
        mem, mem, wbr(1), wbr(2),
    ]
    out_shape = [jax.ShapeDtypeStruct((n, ATTN_W), F32)] * 4 + [
        jax.ShapeDtypeStruct((n, D_MODEL), F32), jax.ShapeDtypeStruct((n, D_MODEL), F32),
        jax.ShapeDtypeStruct((dec_batch, CONV_HALO, CONV_CH), F32)]
    out_specs = [full(ATTN_W)] * 4 + [full(D_MODEL), full(D_MODEL),
                                      pl.BlockSpec((SEQS_PER_STEP, CONV_HALO, CONV_CH), lambda b: (b, 0, 0))]
    scr = lambda w: pltpu.VMEM((n, w), F32)
    return pl.pallas_call(
        functools.partial(_proj_sample_kernel, dec_seq=dec_seq),
        grid=(dec_batch // SEQS_PER_STEP,),
        in_specs=in_specs,
        out_specs=out_specs,
        out_shape=out_shape,
        scratch_shapes=[scr(CONV_CH), scr(CONV_CH), scr(MEM_W), scr(MEM_W), scr(2 * D_MODEL),
                        scr(CONV_CH), scr(MEM_W),
                        pltpu.VMEM((SEQS_PER_STEP, CONV_HALO + dec_seq + 2, CONV_CH), F32)],
        compiler_params=pltpu.CompilerParams(dimension_semantics=("arbitrary",),
                                             vmem_limit_bytes=VMEM_LIMIT),
        name="proj_sample",
    )(x2d, g, w_in_b, *tabs, cw, cb, lng, lnb, state_conv, cache_mem_k, cache_mem_v, w_br_b, w_br_b)


def _page_copies(cache_ref, pt_ref, page_buf, page_sem, layer, sb, chunk, *, start):
    for r in range(FUSED_PAGES):
        page = pt_ref[sb, chunk * FUSED_PAGES + r]
        copy = pltpu.make_async_copy(cache_ref.at[layer, page], page_buf.at[r], page_sem.at[r])
        if start:
            copy.start()
        else:
            copy.wait()


def _head_rows(x, dec_seq):
    rows = N_HEADS * dec_seq
    tiled = jnp.concatenate([x] * N_HEADS, axis=0)
    row_h = lax.broadcasted_iota(jnp.int32, (rows, ATTN_W), 0) // dec_seq
    lane_h = lax.broadcasted_iota(jnp.int32, (rows, ATTN_W), 1) // HEAD_DIM
    return jnp.where(row_h == lane_h, tiled, 0.0)


def _sample_logits_pages(q_ref, page_buf, logit_scr, qb_scr, c, *, dec_seq):
    @pl.when(c == 0)
    def _():
        qb_scr[...] = (_head_rows(q_ref[...], dec_seq) * ATTN_SCALE).astype(BF16)

    qb = qb_scr[...]
    for r0 in range(0, FUSED_PAGES, PAGE_GROUP):
        kt = jnp.concatenate([page_buf[r0 + g].reshape(ATTN_W, PAGE_SIZE).astype(BF16)
                              for g in range(PAGE_GROUP)], axis=1)
        lg = jnp.dot(qb, kt, preferred_element_type=F32)
        for g in range(PAGE_GROUP):
            logit_scr[c * FUSED_PAGES + r0 + g] = lg[:, g * PAGE_SIZE:(g + 1) * PAGE_SIZE]


def _sample_logits_finish(kn_ref, logit_scr, qb_scr, p_out, l_out, c, *, dec_seq, n_pages):
    nc = n_pages // FUSED_PAGES
    rows = N_HEADS * dec_seq
    pages_per_blk = MOBA_BLOCK // PAGE_SIZE
    nblk = n_pages // pages_per_blk

    @pl.when(c == nc - 1)
    def _():
        qb = qb_scr[...]
        blk_idx = lax.broadcasted_iota(jnp.int32, (rows, nblk), 1)
        sc = jnp.zeros((rows, nblk), F32)
        for j in range(nblk):
            blk = logit_scr[j * pages_per_blk]
            for pp in range(1, pages_per_blk):
                blk = blk + logit_scr[j * pages_per_blk + pp]
            sc = jnp.where(blk_idx == j, jnp.sum(blk, axis=-1, keepdims=True), sc)
        sel = _top_k_select(sc, blk_idx >= 0, blk_idx, nblk, axis=1)
        sel_t = jnp.where(sel, 1.0, 0.0)

        kn_page = jnp.concatenate([kn_ref[...], jnp.zeros((PAGE_SIZE - dec_seq, ATTN_W), F32)], axis=0)
        ln = lax.dot_general(qb, kn_page.astype(BF16), _NT, preferred_element_type=F32)
        key_i = lax.broadcasted_iota(jnp.int32, (rows, PAGE_SIZE), 1)
        qry_i = lax.broadcasted_iota(jnp.int32, (rows, PAGE_SIZE), 0) % dec_seq
        ln = jnp.where(key_i <= qry_i, ln, -jnp.inf)

        mx = ln
        for p in range(n_pages):
            j = p // pages_per_blk
            lp = jnp.where(sel_t[:, j:j + 1] > 0.0, logit_scr[p], -jnp.inf)
            logit_scr[p] = lp
            mx = jnp.maximum(mx, lp)
        m = jnp.max(mx, axis=-1, keepdims=True)
        pn = jnp.exp(ln - m)
        p_out[n_pages] = pn.astype(BF16)
        lsum = pn
        for p in range(n_pages):
            pp = jnp.exp(logit_scr[p] - m)
            p_out[p] = pp.astype(BF16)
            lsum = lsum + pp
        l_out[...] = jnp.broadcast_to(jnp.sum(lsum, axis=-1, keepdims=True), (rows, LANES))


def _sample_values_step(p_ref, pn_ref, l_ref, vn_ref, page_buf, acc_scr, o_ref, c, *, dec_seq, n_chunks):
    @pl.when(c == 0)
    def _():
        vn_page = jnp.concatenate([vn_ref[...], jnp.zeros((PAGE_SIZE - dec_seq, ATTN_W), F32)], axis=0)
        acc_scr[...] = jnp.dot(pn_ref[...], vn_page.astype(BF16), preferred_element_type=F32)

    acc = acc_scr[...]
    for r0 in range(0, FUSED_PAGES, VALUE_GROUP):
        vt = jnp.concatenate([page_buf[r0 + g].reshape(ATTN_W, PAGE_SIZE).astype(BF16)
                              for g in range(VALUE_GROUP)], axis=1)
        pw = jnp.concatenate([p_ref[r0 + g] for g in range(VALUE_GROUP)], axis=1)
        acc = acc + lax.dot_general(pw, vt, _NT, preferred_element_type=F32)
    acc_scr[...] = acc

    @pl.when(c == n_chunks - 1)
    def _():
        res = acc / jnp.concatenate([l_ref[...]] * (ATTN_W // LANES), axis=1)
        lane_h = lax.broadcasted_iota(jnp.int32, (dec_seq, ATTN_W), 1) // HEAD_DIM
        out = jnp.zeros((dec_seq, ATTN_W), F32)
        for h in range(N_HEADS):
            out = out + jnp.where(lane_h == h, res[h * dec_seq:(h + 1) * dec_seq, :], 0.0)
        o_ref[...] = out


def _merge_sample_kernel(a_ref, sa_ref, g0_ref, part_ref, x_ref, wb0_ref, wout_ref, gf_ref, o_ref, *, final):
    xn = _merge_out(a_ref[...], sa_ref[...], g0_ref[...], part_ref[...], x_ref[...], wb0_ref[...],
                    wout_ref[...])
    if final:
        xn = _rms_norm(xn, gf_ref[...])
    o_ref[...] = xn


def _merge_sample(attn, sa, g0, part, x2d, w_br_b, w_out_b, gf, layer, final):
    n = x2d.shape[0]
    full = lambda w: _const_spec((n, w))
    return pl.pallas_call(
        functools.partial(_merge_sample_kernel, final=final),
        grid=(1,),
        in_specs=[full(ATTN_W), full(ATTN_W), full(D_MODEL), full(D_MODEL), full(D_MODEL),
                  pl.BlockSpec((None, None, ATTN_W, D_MODEL), lambda i: (layer, 0, 0, 0)),
                  pl.BlockSpec((None, D_MODEL, D_MODEL), lambda i: (layer, 0, 0)), _const_spec((1, D_MODEL))],
        out_specs=full(D_MODEL),
        out_shape=jax.ShapeDtypeStruct((n, D_MODEL), F32),
        compiler_params=pltpu.CompilerParams(dimension_semantics=("arbitrary",),
                                             vmem_limit_bytes=VMEM_LIMIT),
        name="merge_sample",
    )(attn, sa, g0, part, x2d, w_br_b, w_out_b, gf)


def kernel(x_prompt, x_sample, cache_k, cache_v, cache_mem_k, cache_mem_v, state_conv, page_table, mem_prompt,
           g_norm, w_in, conv_w, conv_b, ln_g, ln_b, w_mem_k, w_mem_v, w_branch, w_out, g_final):
    batch, seq, _ = x_prompt.shape
    dec_batch, dec_seq, _ = x_sample.shape
    depth = w_in.shape[0]
    past_len = page_table.shape[1] * PAGE_SIZE
    assert seq % ROW_TILE == 0 and ROW_TILE == MOBA_BLOCK
    assert dec_batch * dec_seq == ROW_TILE and dec_seq == 8 and past_len % MOBA_BLOCK == 0
    assert page_table.shape[1] % FUSED_PAGES == 0

    w_in_b = w_in.astype(BF16)
    w_br_b = w_branch.astype(BF16)
    w_out_b = w_out.astype(BF16)
    gf = g_final.reshape(1, D_MODEL)

    mem_k_p, mem_v_p = _mem_proj(mem_prompt.reshape(batch * MEM_LEN, D_MODEL), w_mem_k, w_mem_v)

    tabs_p = _rope_tables(jnp.arange(seq, dtype=jnp.int32))
    tabs_s = tuple(jnp.tile(t, (dec_batch, 1))
                   for t in _rope_tables(past_len + jnp.arange(dec_seq, dtype=jnp.int32)))

    cache_kt = cache_k.transpose(0, 1, 3, 4, 2)
    cache_vt = cache_v.transpose(0, 1, 3, 4, 2)

    xp = x_prompt.reshape(batch * seq, D_MODEL)
    xs = x_sample.reshape(dec_batch * dec_seq, D_MODEL)
    kv_p = None
    cp_l, ks_l, vs_l, cs_l = [], [], [], []
    for l in range(depth):
        g = g_norm[l].reshape(1, D_MODEL)
        cw, cb = conv_w[l], conv_b[l].reshape(1, CONV_CH)
        lng, lnb = ln_g[l].reshape(1, CONV_CH), ln_b[l].reshape(1, CONV_CH)
        final = l == depth - 1

        qs, k_s, v_s, sas, g0s, parts, csts = _proj_sample(
            xs, g, w_in_b, tabs_s, cw, cb, lng, lnb, state_conv, cache_mem_k, cache_mem_v, w_br_b,
            l, dec_batch, dec_seq)
        q, kt, vt, kb, vtb, km, sa, g0, part, cst, p_s, l_s = _proj_prompt(
            page_table, xp, g, w_in_b, tabs_p, cw, cb, lng, lnb, mem_k_p, mem_v_p, w_br_b, qs, k_s, cache_kt,
            kv_p, l, batch, seq, dec_seq)
        xp, attn_s = _attn_prompt(
            page_table, q, kb, vtb, km.reshape(batch, seq // MOBA_BLOCK, ATTN_W), sa, g0, part, xp,
            w_br_b, w_out_b, gf, p_s, l_s, v_s, cache_vt, l, batch, seq, dec_seq, final)
        kv_p = (kt, vt)
        cp_l.append(cst)
        xs = _merge_sample(attn_s, sas, g0s, parts, xs, w_br_b, w_out_b, gf, l, final)
        ks_l.append(k_s); vs_l.append(v_s); cs_l.append(csts)

    y_prompt = xp.reshape(batch, seq, D_MODEL)
    y_sample = xs.reshape(dec_batch, dec_seq, D_MODEL)
    head_p = lambda t: t.reshape(depth, batch, N_HEADS, HEAD_DIM, seq).transpose(0, 1, 4, 2, 3)
    head_s = lambda ts: jnp.stack(ts).reshape(depth, dec_batch, dec_seq, N_HEADS, HEAD_DIM)
    mem_shape = (depth, batch, MEM_LEN, MEM_HEADS, MEM_HEAD_DIM)
    return (y_prompt, y_sample, head_p(kv_p[0]), head_p(kv_p[1]), jnp.stack(cp_l),
            mem_k_p.reshape(mem_shape), mem_v_p.reshape(mem_shape),
            head_s(ks_l), head_s(vs_l), jnp.stack(cs_l))
```

```python
import functools

import jax
import jax.numpy as jnp
from jax import lax
from jax.experimental import pallas as pl
from jax.experimental.pallas import tpu as pltpu

F32 = jnp.float32
BF16 = jnp.bfloat16

D_MODEL = 1024
N_HEADS = 8
HEAD_DIM = 64
ATTN_W = N_HEADS * HEAD_DIM
ROT_DIM = HEAD_DIM // 4
ROPE_THETA = 500000.0
MOBA_BLOCK = 256
MOBA_TOP_K = 3
CONV_CH = 512
CONV_K = 31
CONV_HALO = CONV_K - 1
MEM_LEN = 256
MEM_HEADS = 4
MEM_HEAD_DIM = 128
MEM_W = MEM_HEADS * MEM_HEAD_DIM
PAGE_SIZE = 128
RMS_EPS = 1e-6
LN_EPS = 1e-5
ATTN_SCALE = HEAD_DIM ** -0.5
MEM_SCALE = MEM_HEAD_DIM ** -0.5

LANES = 128
SUBLANES = 8
LOG2E = 1.4426950408889634
MASK_BIAS = -1e30
MAX_FLOOR = -1e29
HEAD_ROWS = HEAD_DIM + 16
ROW_TILE = 256
LOGITS_AHEAD = 4
FUSED_PAGES = 32
PAGE_GROUP = 2
VALUE_GROUP = 8
SEQS_PER_STEP = 4
CONV_ROWS = 32
HALO_PAD = 32
VMEM_LIMIT = 56 * 1024 * 1024

_Q, _K, _V, _AG, _CV, _CG, _CGATE, _MQ, _MG, _MERGE = (
    0, 512, 1024, 1536, 2048, 2560, 3072, 3584, 4096, 4608)
N_IN = _MERGE + 3 * D_MODEL

_NT = (((1,), (1,)), ((), ()))


def _rms_norm(x, g):
    return x * lax.rsqrt(jnp.mean(x * x, axis=-1, keepdims=True) + RMS_EPS) * g


def _layer_norm(x, g, b):
    mu = jnp.mean(x, axis=-1, keepdims=True)
    xc = x - mu
    var = jnp.mean(xc * xc, axis=-1, keepdims=True)
    return xc * lax.rsqrt(var + LN_EPS) * g + b


def _rope(xc, cosf, sina, sinb):
    return (xc * cosf + pltpu.roll(xc, LANES - ROT_DIM // 2, 1) * sina
            + pltpu.roll(xc, ROT_DIM // 2, 1) * sinb)


def _rope_tables(pos):
    half = ROT_DIM // 2
    inv = ROPE_THETA ** (-jnp.arange(0, ROT_DIM, 2, dtype=F32) / ROT_DIM)
    ang = pos.astype(F32)[:, None] * inv[None, :]
    cos, sin = jnp.cos(ang), jnp.sin(ang)
    n = pos.shape[0]
    zeros_h = jnp.zeros((n, half), F32)
    rest0 = jnp.zeros((n, HEAD_DIM - ROT_DIM), F32)
    cosf = jnp.concatenate([cos, cos, jnp.ones((n, HEAD_DIM - ROT_DIM), F32)], axis=1)
    sina = jnp.concatenate([-sin, zeros_h, rest0], axis=1)
    sinb = jnp.concatenate([zeros_h, sin, rest0], axis=1)
    rep = LANES // HEAD_DIM
    return tuple(jnp.tile(t, (1, rep)) for t in (cosf, sina, sinb))


def _top_k_select(sc, valid, idx, n, axis=0):
    scm = jnp.where(valid, sc, -jnp.inf)
    rank = jnp.zeros(sc.shape, F32)
    for j in range(n):
        sj = scm[j:j + 1, :] if axis == 0 else scm[:, j:j + 1]
        beats = (sj > scm) | ((sj == scm) & (j < idx))
        rank = rank + jnp.where(beats, 1.0, 0.0)
    return valid & (rank < float(MOBA_TOP_K))


def _zero_after(x, zero_bits):
    rows, cols = zero_bits.shape
    bits = pltpu.bitcast(x, jnp.int32)
    acc = bits[0:rows, 0:cols]
    for r in range(0, x.shape[0], rows):
        for c in range(0, x.shape[1], cols):
            if r or c:
                acc = acc | bits[r:r + rows, c:c + cols]
    return pltpu.bitcast(acc & zero_bits, F32)


def _block_lane(h):
    return HEAD_DIM if h % 2 == 0 else 0


def _split_bf16(x):
    hi = x.astype(BF16)
    return hi, (x - hi.astype(F32)).astype(BF16)


def _merge_out(attn, sa, g0, part, x, wb0, wout):
    a = (attn * sa).astype(BF16)
    pa = jnp.dot(a, wb0, preferred_element_type=F32)
    mix = (g0 * pa + part).astype(BF16)
    return x + jnp.dot(mix, wout, preferred_element_type=F32)


def _mem_proj_kernel(mem_ref, wk_ref, wv_ref, mk_out, mv_out):
    mb = mem_ref[...].astype(BF16)
    mk_out[...] = jnp.dot(mb, wk_ref[...].astype(BF16), preferred_element_type=F32)
    mv_out[...] = jnp.dot(mb, wv_ref[...].astype(BF16), preferred_element_type=F32)


def _mem_proj(mem2d, w_mem_k, w_mem_v):
    depth = w_mem_k.shape[0]
    rows = mem2d.shape[0]
    w_spec = pl.BlockSpec((None, D_MODEL, MEM_W), lambda l: (l, 0, 0))
    o_spec = pl.BlockSpec((None, rows, MEM_W), lambda l: (l, 0, 0))
    return pl.pallas_call(
        _mem_proj_kernel,
        grid=(depth,),
        in_specs=[pl.BlockSpec((rows, D_MODEL), lambda l: (0, 0)), w_spec, w_spec],
        out_specs=[o_spec, o_spec],
        out_shape=[jax.ShapeDtypeStruct((depth, rows, MEM_W), F32)] * 2,
        compiler_params=pltpu.CompilerParams(dimension_semantics=("arbitrary",),
                                             vmem_limit_bytes=VMEM_LIMIT),
        name="mem_proj",
    )(mem2d, w_mem_k, w_mem_v)


def _proj_prompt_kernel(pt_ref, x_ref, g_ref, w_ref, cos_ref, sina_ref, sinb_ref, cw_ref, cb_ref, lng_ref, lnb_ref,
                        mk_ref, mv_ref, wb1_ref, wb2_ref, qs_ref, kn_ref, zero_ref, cache_ref, *refs,
                        tiles_per_seq, n_prev, layer, dec_seq, n_pages):
    if n_prev:
        ktp_ref, vtp_ref = refs[:2]
        refs = refs[2:]
    (q_out, kt_out, vt_out, kb_out, vtb_out, km_out, sa_out, g0_out, part_out, cst_out, p_out, l_out,
     ubuf, urot, hb_scr, conv_scr, cgate_scr, mq_scr, mg_scr, page_buf, page_sem, logit_scr, qb_scr) = refs
    tm = ROW_TILE
    step = pl.program_id(0)
    tin = step % tiles_per_seq
    steps_per_seq = n_pages // FUSED_PAGES
    sb, chunk = step // steps_per_seq, step % steps_per_seq
    _page_copies(cache_ref, pt_ref, page_buf, page_sem, layer, sb, chunk, start=True)
    if n_prev:
        kt_out[0:n_prev] = ktp_ref[...]
        vt_out[0:n_prev] = vtp_ref[...]
    hb_scr[...] = _rms_norm(x_ref[...], g_ref[...]).astype(BF16)

    def seg(a, width, after=None):
        if after is not None:
            zero = _zero_after(after, zero_ref[...]).astype(BF16)
            for r in range(0, tm, 2 * SUBLANES):
                hb_scr[r:r + 2 * SUBLANES, 0:LANES] = hb_scr[r:r + 2 * SUBLANES, 0:LANES] + zero
        return jnp.dot(hb_scr[...], w_ref[:, a:a + width], preferred_element_type=F32)

    cosf, sina, sinb = cos_ref[...], sina_ref[...], sinb_ref[...]

    u = seg(_CV, CONV_CH) * jax.nn.sigmoid(seg(_CG, CONV_CH))

    @pl.when(tin == 0)
    def _():
        ubuf[0:HALO_PAD, :] = jnp.zeros((HALO_PAD, CONV_CH), F32)

    @pl.when(tin != 0)
    def _():
        ubuf[HALO_PAD - CONV_HALO:HALO_PAD, :] = ubuf[tm + HALO_PAD - CONV_HALO:tm + HALO_PAD, :]

    ubuf[HALO_PAD:HALO_PAD + tm, :] = u
    nrot = tm + HALO_PAD - SUBLANES
    for r in range(1, SUBLANES):
        urot[r - 1] = ubuf[r:r + nrot, :]
    cst_out[...] = ubuf[tm + HALO_PAD - CONV_HALO:tm + HALO_PAD, :]

    def conv_rows(k):
        acc = jnp.broadcast_to(cb_ref[...], (CONV_ROWS, CONV_CH))
        for t in range(CONV_K):
            a, r = divmod(HALO_PAD - CONV_HALO + t, SUBLANES)
            rows = slice(a * SUBLANES + k * CONV_ROWS, a * SUBLANES + (k + 1) * CONV_ROWS)
            acc = acc + cw_ref[t:t + 1, :] * (ubuf[rows, :] if r == 0 else urot[r - 1, rows, :])
        conv_scr[k * CONV_ROWS:(k + 1) * CONV_ROWS, :] = acc
        return acc

    def q_task(after):
        zq = seg(_Q, ATTN_W, after)
        for c in range(ATTN_W // LANES):
            sl = slice(c * LANES, (c + 1) * LANES)
            q_out[:, sl] = _rope(zq[:, sl], cosf, sina, sinb)

    def k_task(after):
        zk = seg(_K, ATTN_W, after)
        lane = lax.broadcasted_iota(jnp.int32, (tm, LANES), 1)
        for c in range(ATTN_W // LANES):
            sl = slice(c * LANES, (c + 1) * LANES)
            kr = _rope(zk[:, sl], cosf, sina, sinb)
            kt_out[n_prev, sl, :] = kr.T
            for hh in range(2):
                onehot = jnp.where(lane == _block_lane(2 * c + hh) + tin, 1.0, 0.0)
                kb_out[:, (2 * c + hh) * LANES:(2 * c + hh + 1) * LANES] = jnp.where(
                    lane // HEAD_DIM == hh, kr, onehot).astype(BF16)
            km_out[:, sl] = jnp.mean(kr, axis=0, keepdims=True)

    def v_task(after):
        zvt = seg(_V, ATTN_W, after).T
        vt_out[n_prev] = zvt
        for h in range(N_HEADS):
            vtb_out[h * HEAD_ROWS:h * HEAD_ROWS + HEAD_DIM, :] = (
                zvt[h * HEAD_DIM:(h + 1) * HEAD_DIM, :].astype(BF16))
            vtb_out[h * HEAD_ROWS + HEAD_DIM:(h + 1) * HEAD_ROWS, :] = jnp.ones((HEAD_ROWS - HEAD_DIM, tm), BF16)

    def gate_task(after):
        sa_out[...] = jax.nn.silu(seg(_AG, ATTN_W, after))

    def cgate_task(after):
        cgate_scr[...] = jax.nn.silu(seg(_CGATE, CONV_CH, after))

    def mq_task(after):
        mq_scr[...] = seg(_MQ, MEM_W, after).astype(BF16)

    def mg_task(after):
        mg_scr[...] = jax.nn.silu(seg(_MG, MEM_W, after))

    def g0_task(after):
        g0_out[...] = jax.nn.sigmoid(seg(_MERGE, D_MODEL, after))

    tasks = [q_task, k_task, v_task, gate_task, cgate_task, mq_task, mg_task, g0_task]
    assert len(tasks) == tm // CONV_ROWS
    after = None
    for k, task in enumerate(tasks):
        task(after)
        after = conv_rows(k)
    cbr = jax.nn.silu(_layer_norm(conv_scr[...], lng_ref[...], lnb_ref[...])) * cgate_scr[...]

    mparts = []
    for hd in range(MEM_HEADS):
        sl = slice(hd * MEM_HEAD_DIM, (hd + 1) * MEM_HEAD_DIM)
        s = lax.dot_general(mq_scr[:, sl], mk_ref[:, sl].astype(BF16), _NT,
                            preferred_element_type=F32) * MEM_SCALE
        p = jnp.exp(s - jnp.max(s, axis=-1, keepdims=True))
        o = jnp.dot(p.astype(BF16), mv_ref[:, sl].astype(BF16), preferred_element_type=F32)
        mparts.append(o / jnp.sum(p, axis=-1, keepdims=True))
    mbr = jnp.concatenate(mparts, axis=1) * mg_scr[...]

    pc = jnp.dot(cbr.astype(BF16), wb1_ref[...], preferred_element_type=F32)
    pm = jnp.dot(mbr.astype(BF16), wb2_ref[...], preferred_element_type=F32)
    part_out[...] = (jax.nn.sigmoid(seg(_MERGE + D_MODEL, D_MODEL)) * pc
                     + jax.nn.sigmoid(seg(_MERGE + 2 * D_MODEL, D_MODEL)) * pm)

    _page_copies(cache_ref, pt_ref, page_buf, page_sem, layer, sb, chunk, start=False)
    _sample_logits_pages(qs_ref, page_buf, logit_scr, qb_scr, chunk, dec_seq=dec_seq)
    _sample_logits_finish(kn_ref, logit_scr, qb_scr, p_out, l_out, chunk, dec_seq=dec_seq, n_pages=n_pages)


def _const_spec(shape, ngrid=1):
    zeros = (0,) * len(shape)
    if ngrid == 1:
        return pl.BlockSpec(shape, lambda i: zeros)
    return pl.BlockSpec(shape, lambda i, j: zeros)


def _proj_prompt(page_table, x2d, g, w_in_b, tabs, cw, cb, lng, lnb, mk, mv, w_br_b, q_s, k_s, cache_kt,
                 kv_prev, layer, batch, seq, dec_seq):
    tm = ROW_TILE
    n = batch * seq
    tps = seq // tm
    nt = n // tm
    dec_batch, n_pages = page_table.shape
    sps = n_pages // FUSED_PAGES
    assert nt == dec_batch * sps
    rows_s = N_HEADS * dec_seq
    n_prev = 0 if kv_prev is None else kv_prev[0].shape[0]
    const = lambda shape: pl.BlockSpec(shape, lambda t, pt: (0,) * len(shape))
    row = lambda w: pl.BlockSpec((tm, w), lambda t, pt: (t, 0))
    tab = pl.BlockSpec((tm, LANES), lambda t, pt: (t % tps, 0))
    mem = pl.BlockSpec((None, MEM_LEN, MEM_W), lambda t, pt: (layer, t // tps, 0))
    wbr = lambda br: pl.BlockSpec((None, None, ATTN_W, D_MODEL), lambda t, pt: (layer, br, 0, 0))
    seq_rows = pl.BlockSpec((dec_seq, ATTN_W), lambda t, pt: (t // sps, 0))
    in_specs = [
        row(D_MODEL), const((1, D_MODEL)),
        pl.BlockSpec((None, D_MODEL, N_IN), lambda t, pt: (layer, 0, 0), pipeline_mode=pl.Buffered(1)),
        tab, tab, tab,
        const((CONV_K, CONV_CH)), const((1, CONV_CH)), const((1, CONV_CH)), const((1, CONV_CH)),
        mem, mem, wbr(1), wbr(2),
        seq_rows, seq_rows, const((2 * SUBLANES, LANES)), pl.BlockSpec(memory_space=pl.ANY),
    ]
    zero_bits = jnp.zeros((2 * SUBLANES, LANES), jnp.int32)
    args = [page_table, x2d, g, w_in_b, *tabs, cw, cb, lng, lnb, mk, mv, w_br_b, w_br_b, q_s, k_s, zero_bits,
            cache_kt]
    if n_prev:
        prev = pl.BlockSpec((n_prev, None, ATTN_W, tm), lambda t, pt: (0, t // tps, 0, t % tps))
        in_specs += [prev, prev]
        args += list(kv_prev)
    out_shape = [
        jax.ShapeDtypeStruct((n, ATTN_W), F32),
        jax.ShapeDtypeStruct((n_prev + 1, batch, ATTN_W, seq), F32),
        jax.ShapeDtypeStruct((n_prev + 1, batch, ATTN_W, seq), F32),
        jax.ShapeDtypeStruct((nt, tm, N_HEADS * LANES), BF16),
        jax.ShapeDtypeStruct((nt, N_HEADS * HEAD_ROWS, tm), BF16),
        jax.ShapeDtypeStruct((nt, 1, ATTN_W), F32),
        jax.ShapeDtypeStruct((n, ATTN_W), F32),
        jax.ShapeDtypeStruct((n, D_MODEL), F32),
        jax.ShapeDtypeStruct((n, D_MODEL), F32),
        jax.ShapeDtypeStruct((batch, CONV_HALO, CONV_CH), F32),
        jax.ShapeDtypeStruct((dec_batch, n_pages + 1, rows_s, PAGE_SIZE), BF16),
        jax.ShapeDtypeStruct((dec_batch, rows_s, LANES), F32),
    ]
    seq_t = pl.BlockSpec((n_prev + 1, None, ATTN_W, tm), lambda t, pt: (0, t // tps, 0, t % tps))
    out_specs = [
        row(ATTN_W), seq_t, seq_t,
        pl.BlockSpec((None, tm, N_HEADS * LANES), lambda t, pt: (t, 0, 0)),
        pl.BlockSpec((None, N_HEADS * HEAD_ROWS, tm), lambda t, pt: (t, 0, 0)),
        pl.BlockSpec((None, 1, ATTN_W), lambda t, pt: (t, 0, 0)),
        row(ATTN_W), row(D_MODEL), row(D_MODEL),
        pl.BlockSpec((None, CONV_HALO, CONV_CH), lambda t, pt: (t // tps, 0, 0)),
        pl.BlockSpec((None, n_pages + 1, rows_s, PAGE_SIZE), lambda t, pt: (t // sps, 0, 0, 0)),
        pl.BlockSpec((None, rows_s, LANES), lambda t, pt: (t // sps, 0, 0)),
    ]
    return pl.pallas_call(
        functools.partial(_proj_prompt_kernel, tiles_per_seq=tps, n_prev=n_prev, layer=layer,
                          dec_seq=dec_seq, n_pages=n_pages),
        grid_spec=pltpu.PrefetchScalarGridSpec(
            num_scalar_prefetch=1,
            grid=(nt,),
            in_specs=in_specs,
            out_specs=out_specs,
            scratch_shapes=[pltpu.VMEM((tm + HALO_PAD, CONV_CH), F32),
                            pltpu.VMEM((SUBLANES - 1, tm + HALO_PAD - SUBLANES, CONV_CH), F32),
                            pltpu.VMEM((tm, D_MODEL), BF16),
                            pltpu.VMEM((tm, CONV_CH), F32),
                            pltpu.VMEM((tm, CONV_CH), F32),
                            pltpu.VMEM((tm, MEM_W), BF16),
                            pltpu.VMEM((tm, MEM_W), F32),
                            pltpu.VMEM((FUSED_PAGES, N_HEADS, HEAD_DIM, PAGE_SIZE), F32),
                            pltpu.SemaphoreType.DMA((FUSED_PAGES,)),
                            pltpu.VMEM((n_pages, rows_s, PAGE_SIZE), F32),
                            pltpu.VMEM((rows_s, ATTN_W), BF16)]),
        out_shape=out_shape,
        compiler_params=pltpu.CompilerParams(dimension_semantics=("arbitrary",),
                                             vmem_limit_bytes=VMEM_LIMIT),
        name="proj_prompt",
    )(*args)


def _attn_prompt_kernel(pt_ref, q_ref, kb_ref, vt_ref, km_ref, sa_ref, g0_ref, part_ref, x_ref, wb0_ref, wout_ref,
                        gf_ref, ps_ref, pns_ref, ls_ref, vn_ref, cache_ref, o_ref, so_ref,
                        sc_scr, sel_scr, qb_scr, m_scr, acc_scr, page_buf, page_sem, sacc_scr,
                        *, final, nblk, layer, dec_seq, n_pages):
    tq = ROW_TILE
    i = pl.program_id(1)
    step = pl.program_id(0) * nblk + i
    steps_per_seq = n_pages // FUSED_PAGES
    sb, chunk = step // steps_per_seq, step % steps_per_seq
    _page_copies(cache_ref, pt_ref, page_buf, page_sem, layer, sb, chunk, start=True)
    blk_idx = lax.broadcasted_iota(jnp.int32, (nblk, tq), 0)
    valid = blk_idx < i
    lane = lax.broadcasted_iota(jnp.int32, (tq, LANES), 1)
    causal = (lax.broadcasted_iota(jnp.int32, (MOBA_BLOCK, tq), 0)
              <= lax.broadcasted_iota(jnp.int32, (MOBA_BLOCK, tq), 1))
    pair = lambda h: slice((h // 2) * LANES, (h // 2 + 1) * LANES)
    col = lambda h: slice(h * LANES, (h + 1) * LANES)
    rows = lambda h: slice(h * HEAD_ROWS, (h + 1) * HEAD_ROWS)
    bcast = lambda r: jnp.broadcast_to(r, (SUBLANES, tq))
    own_half = lambda h: jnp.where((lane // HEAD_DIM) == h % 2, q_ref[:, pair(h)], 0.0)

    for h in range(N_HEADS):
        qm = own_half(h)
        q_hi, q_lo = _split_bf16(qm)
        km_hi, km_lo = _split_bf16(km_ref[:, pair(h)])
        sc = lax.dot_general(jnp.concatenate([km_hi, km_hi, km_lo], axis=1),
                             jnp.concatenate([q_hi, q_lo, q_hi], axis=1), _NT,
                             preferred_element_type=F32)
        sc_scr[h] = jnp.where(valid, sc, -jnp.inf)
        sel_scr[h] = jnp.zeros((nblk, tq), F32)
        m_scr[h] = jnp.full((SUBLANES, tq), MAX_FLOOR, F32)
    acc_scr[...] = jnp.zeros((N_HEADS * HEAD_ROWS, tq), F32)

    def rank_body(j, carry):
        for h in range(N_HEADS):
            scm = sc_scr[h]
            sj = sc_scr[h, pl.ds(j, 1), :]
            before = (sj > scm) | ((sj == scm) & (j < blk_idx))
            sel_scr[h] = sel_scr[h] + jnp.where(before, 1.0, 0.0)
        return carry

    lax.fori_loop(0, i, rank_body, 0)
    for h in range(N_HEADS):
        keep = (valid & (sel_scr[h] < float(MOBA_TOP_K))) | (blk_idx == i)
        pieces = [jnp.where(keep, 0.0, MASK_BIAS)]
        if _block_lane(h):
            pieces.insert(0, jnp.zeros((_block_lane(h), tq), F32))
        pieces.append(jnp.zeros((LANES - _block_lane(h) - nblk, tq), F32))
        bias = jnp.concatenate(pieces, axis=0).T
        qb_scr[h] = (own_half(h) * (ATTN_SCALE * LOG2E) + bias).astype(BF16)

    def logits(j, h):
        return lax.dot_general(kb_ref[j, :, col(h)], qb_scr[h], _NT, preferred_element_type=F32)

    def softmax_step(j, h, s):
        m_old = m_scr[h, 0:1, :]
        m_new = jnp.maximum(m_old, jnp.max(s, axis=0, keepdims=True))
        alpha = jnp.exp2(m_old - m_new)
        p = jnp.exp2(s - m_new)
        m_scr[h] = bcast(m_new)
        acc_scr[rows(h), :] = alpha * acc_scr[rows(h), :] + jnp.dot(
            vt_ref[j, rows(h), :], p.astype(BF16), preferred_element_type=F32)

    def body(j, ahead):
        ahead = list(ahead)
        for h in range(N_HEADS):
            s = ahead.pop(0)
            nh = h + LOGITS_AHEAD
            ahead.append(logits(j, nh) if nh < N_HEADS else logits(j + 1, nh - N_HEADS))
            softmax_step(j, h, s)
        return tuple(ahead)

    ahead = list(lax.fori_loop(0, i, body, tuple(logits(0, h) for h in range(LOGITS_AHEAD))))
    for h in range(N_HEADS):
        s = ahead.pop(0)
        if h + LOGITS_AHEAD < N_HEADS:
            ahead.append(logits(i, h + LOGITS_AHEAD))
        softmax_step(i, h, jnp.where(causal, s, -jnp.inf))
    outs = []
    for h in range(N_HEADS):
        blk = acc_scr[rows(h), :]
        outs.append(blk[0:HEAD_DIM, :] / blk[HEAD_DIM:HEAD_DIM + 1, :])

    attn = jnp.concatenate(outs, axis=0).T
    xn = _merge_out(attn, sa_ref[...], g0_ref[...], part_ref[...], x_ref[...], wb0_ref[...], wout_ref[...])
    if final:
        xn = _rms_norm(xn, gf_ref[...])
    o_ref[...] = xn

    _page_copies(cache_ref, pt_ref, page_buf, page_sem, layer, sb, chunk, start=False)
    _sample_values_step(ps_ref, pns_ref, ls_ref, vn_ref, page_buf, sacc_scr, so_ref, chunk,
                        dec_seq=dec_seq, n_chunks=steps_per_seq)


def _attn_prompt(page_table, q, kb, vt, km, sa, g0, part, x2d, w_br_b, w_out_b, gf, p_s, l_s, v_s, cache_vt,
                 layer, batch, seq, dec_seq, final):
    tq = ROW_TILE
    nblk = seq // MOBA_BLOCK
    dec_batch, n_pages = page_table.shape
    sps = n_pages // FUSED_PAGES
    assert batch * nblk == dec_batch * sps
    rows_s = N_HEADS * dec_seq
    sstep = lambda b, i: (b * nblk + i) // sps
    row = lambda w: pl.BlockSpec((tq, w), lambda b, i, pt: (b * nblk + i, 0))
    seq_rows = pl.BlockSpec((dec_seq, ATTN_W), lambda b, i, pt: (sstep(b, i), 0))
    in_specs = [
        row(ATTN_W),
        pl.BlockSpec((nblk, MOBA_BLOCK, N_HEADS * LANES), lambda b, i, pt: (b, 0, 0)),
        pl.BlockSpec((nblk, N_HEADS * HEAD_ROWS, MOBA_BLOCK), lambda b, i, pt: (b, 0, 0)),
        pl.BlockSpec((None, nblk, ATTN_W), lambda b, i, pt: (b, 0, 0)),
        row(ATTN_W), row(D_MODEL), row(D_MODEL), row(D_MODEL),
        pl.BlockSpec((None, None, ATTN_W, D_MODEL), lambda b, i, pt: (layer, 0, 0, 0)),
        pl.BlockSpec((None, D_MODEL, D_MODEL), lambda b, i, pt: (layer, 0, 0)),
        pl.BlockSpec((1, D_MODEL), lambda b, i, pt: (0, 0)),
        pl.BlockSpec((None, FUSED_PAGES, rows_s, PAGE_SIZE),
                     lambda b, i, pt: (sstep(b, i), (b * nblk + i) % sps, 0, 0)),
        pl.BlockSpec((None, None, rows_s, PAGE_SIZE), lambda b, i, pt: (sstep(b, i), n_pages, 0, 0)),
        pl.BlockSpec((None, rows_s, LANES), lambda b, i, pt: (sstep(b, i), 0, 0)),
        seq_rows, pl.BlockSpec(memory_space=pl.ANY),
    ]
    return pl.pallas_call(
        functools.partial(_attn_prompt_kernel, final=final, nblk=nblk, layer=layer, dec_seq=dec_seq,
                          n_pages=n_pages),
        grid_spec=pltpu.PrefetchScalarGridSpec(
            num_scalar_prefetch=1,
            grid=(batch, nblk),
            in_specs=in_specs,
            out_specs=[row(D_MODEL), seq_rows],
            scratch_shapes=[pltpu.VMEM((N_HEADS, nblk, tq), F32),
                            pltpu.VMEM((N_HEADS, nblk, tq), F32),
                            pltpu.VMEM((N_HEADS, tq, LANES), BF16),
                            pltpu.VMEM((N_HEADS, SUBLANES, tq), F32),
                            pltpu.VMEM((N_HEADS * HEAD_ROWS, tq), F32),
                            pltpu.VMEM((FUSED_PAGES, N_HEADS, HEAD_DIM, PAGE_SIZE), F32),
                            pltpu.SemaphoreType.DMA((FUSED_PAGES,)),
                            pltpu.VMEM((rows_s, ATTN_W), F32)]),
        out_shape=[jax.ShapeDtypeStruct((batch * seq, D_MODEL), F32),
                   jax.ShapeDtypeStruct((dec_batch * dec_seq, ATTN_W), F32)],
        compiler_params=pltpu.CompilerParams(dimension_semantics=("arbitrary", "arbitrary"),
                                             vmem_limit_bytes=VMEM_LIMIT),
        name="attn_prompt",
    )(page_table, q, kb, vt, km, sa, g0, part, x2d, w_br_b, w_out_b, gf, p_s, p_s, l_s, v_s, cache_vt)


def _proj_sample_kernel(x_ref, g_ref, w_ref, cos_ref, sina_ref, sinb_ref, cw_ref, cb_ref, lng_ref, lnb_ref,
                        st_ref, mk_ref, mv_ref, wb1_ref, wb2_ref,
                        q_out, k_out, v_out, sa_out, g0_out, part_out, cst_out,
                        u_scr, cgate_scr, mq_scr, mgate_scr, g12_scr, c_scr, m_scr, full_scr, *, dec_seq):
    b = pl.program_id(0)
    nb = pl.num_programs(0)

    @pl.when(b == 0)
    def _():
        hb = _rms_norm(x_ref[...], g_ref[...]).astype(BF16)

        def seg(a, width):
            return jnp.dot(hb, w_ref[:, a:a + width], preferred_element_type=F32)

        cosf, sina, sinb = cos_ref[...], sina_ref[...], sinb_ref[...]
        zq = seg(_Q, ATTN_W)
        zk = seg(_K, ATTN_W)
        for c in range(ATTN_W // LANES):
            sl = slice(c * LANES, (c + 1) * LANES)
            q_out[:, sl] = _rope(zq[:, sl], cosf, sina, sinb)
            k_out[:, sl] = _rope(zk[:, sl], cosf, sina, sinb)
        v_out[...] = seg(_V, ATTN_W)
        sa_out[...] = jax.nn.silu(seg(_AG, ATTN_W))
        u_scr[...] = seg(_CV, CONV_CH) * jax.nn.sigmoid(seg(_CG, CONV_CH))
        cgate_scr[...] = jax.nn.silu(seg(_CGATE, CONV_CH))
        mq_scr[...] = seg(_MQ, MEM_W)
        mgate_scr[...] = jax.nn.silu(seg(_MG, MEM_W))
        g0_out[...] = jax.nn.sigmoid(seg(_MERGE, D_MODEL))
        g12_scr[:, 0:D_MODEL] = jax.nn.sigmoid(seg(_MERGE + D_MODEL, D_MODEL))
        g12_scr[:, D_MODEL:2 * D_MODEL] = jax.nn.sigmoid(seg(_MERGE + 2 * D_MODEL, D_MODEL))

    for sq in range(SEQS_PER_STEP):
        r0 = pl.multiple_of((b * SEQS_PER_STEP + sq) * dec_seq, dec_seq)
        rows = pl.ds(r0, dec_seq)

        full_scr[sq, 0:CONV_HALO, :] = st_ref[sq]
        full_scr[sq, CONV_HALO:CONV_HALO + dec_seq, :] = u_scr[rows, :]
        conv = jnp.broadcast_to(cb_ref[...], (dec_seq, CONV_CH))
        for t in range(CONV_K):
            conv = conv + cw_ref[t:t + 1, :] * full_scr[sq, t:t + dec_seq, :]
        cst_out[sq] = full_scr[sq, dec_seq:dec_seq + CONV_HALO, :]
        c_scr[rows, :] = jax.nn.silu(_layer_norm(conv, lng_ref[...], lnb_ref[...])) * cgate_scr[rows, :]

        mq = mq_scr[rows, :]
        mparts = []
        for hd in range(MEM_HEADS):
            sl = slice(hd * MEM_HEAD_DIM, (hd + 1) * MEM_HEAD_DIM)
            s = lax.dot_general(mq[:, sl], mk_ref[sq, :, hd, :], _NT, preferred_element_type=F32) * MEM_SCALE
            p = jnp.exp(s - jnp.max(s, axis=-1, keepdims=True))
            o = jnp.dot(p, mv_ref[sq, :, hd, :], preferred_element_type=F32)
            mparts.append(o / jnp.sum(p, axis=-1, keepdims=True))
        m_scr[rows, :] = jnp.concatenate(mparts, axis=1) * mgate_scr[rows, :]

    @pl.when(b == nb - 1)
    def _():
        pc = jnp.dot(c_scr[...].astype(BF16), wb1_ref[...], preferred_element_type=F32)
        pm = jnp.dot(m_scr[...].astype(BF16), wb2_ref[...], preferred_element_type=F32)
        part_out[...] = g12_scr[:, 0:D_MODEL] * pc + g12_scr[:, D_MODEL:2 * D_MODEL] * pm


def _proj_sample(x2d, g, w_in_b, tabs, cw, cb, lng, lnb, state_conv, cache_mem_k, cache_mem_v, w_br_b,
                 layer, dec_batch, dec_seq):
    n = dec_batch * dec_seq
    full = lambda w: _const_spec((n, w))
    assert dec_batch % SEQS_PER_STEP == 0
    mem = pl.BlockSpec((None, SEQS_PER_STEP, MEM_LEN, MEM_HEADS, MEM_HEAD_DIM), lambda b: (layer, b, 0, 0, 0))
    wbr = lambda br: pl.BlockSpec((None, None, ATTN_W, D_MODEL), lambda b: (layer, br, 0, 0))
    in_specs = [
        full(D_MODEL), _const_spec((1, D_MODEL)),
        pl.BlockSpec((None, D_MODEL, N_IN), lambda b: (layer, 0, 0), pipeline_mode=pl.Buffered(1)),
        full(LANES), full(LANES), full(LANES),
        _const_spec((CONV_K, CONV_CH)), _const_spec((1, CONV_CH)), _const_spec((1, CONV_CH)),
        _const_spec((1, CONV_CH)),
        pl.BlockSpec((None, SEQS_PER_STEP, CONV_HALO, CONV_CH), lambda b: (layer, b, 0, 0)),
        mem, mem, wbr(1), wbr(2),
    ]
    out_shape = [jax.ShapeDtypeStruct((n, ATTN_W), F32)] * 4 + [
        jax.ShapeDtypeStruct((n, D_MODEL), F32), jax.ShapeDtypeStruct((n, D_MODEL), F32),
        jax.ShapeDtypeStruct((dec_batch, CONV_HALO, CONV_CH), F32)]
    out_specs = [full(ATTN_W)] * 4 + [full(D_MODEL), full(D_MODEL),
                                      pl.BlockSpec((SEQS_PER_STEP, CONV_HALO, CONV_CH), lambda b: (b, 0, 0))]
    scr = lambda w: pltpu.VMEM((n, w), F32)
    return pl.pallas_call(
        functools.partial(_proj_sample_kernel, dec_seq=dec_seq),
        grid=(dec_batch // SEQS_PER_STEP,),
        in_specs=in_specs,
        out_specs=out_specs,
        out_shape=out_shape,
        scratch_shapes=[scr(CONV_CH), scr(CONV_CH), scr(MEM_W), scr(MEM_W), scr(2 * D_MODEL),
                        scr(CONV_CH), scr(MEM_W),
                        pltpu.VMEM((SEQS_PER_STEP, CONV_HALO + dec_seq + 2, CONV_CH), F32)],
        compiler_params=pltpu.CompilerParams(dimension_semantics=("arbitrary",),
                                             vmem_limit_bytes=VMEM_LIMIT),
        name="proj_sample",
    )(x2d, g, w_in_b, *tabs, cw, cb, lng, lnb, state_conv, cache_mem_k, cache_mem_v, w_br_b, w_br_b)


def _page_copies(cache_ref, pt_ref, page_buf, page_sem, layer, sb, chunk, *, start):
    for r in range(FUSED_PAGES):
        page = pt_ref[sb, chunk * FUSED_PAGES + r]
        copy = pltpu.make_async_copy(cache_ref.at[layer, page], page_buf.at[r], page_sem.at[r])
        if start:
            copy.start()
        else:
            copy.wait()


def _head_rows(x, dec_seq):
    rows = N_HEADS * dec_seq
    tiled = jnp.concatenate([x] * N_HEADS, axis=0)
    row_h = lax.broadcasted_iota(jnp.int32, (rows, ATTN_W), 0) // dec_seq
    lane_h = lax.broadcasted_iota(jnp.int32, (rows, ATTN_W), 1) // HEAD_DIM
    return jnp.where(row_h == lane_h, tiled, 0.0)


def _sample_logits_pages(q_ref, page_buf, logit_scr, qb_scr, c, *, dec_seq):
    @pl.when(c == 0)
    def _():
        qb_scr[...] = (_head_rows(q_ref[...], dec_seq) * ATTN_SCALE).astype(BF16)

    qb = qb_scr[...]
    for r0 in range(0, FUSED_PAGES, PAGE_GROUP):
        kt = jnp.concatenate([page_buf[r0 + g].reshape(ATTN_W, PAGE_SIZE).astype(BF16)
                              for g in range(PAGE_GROUP)], axis=1)
        lg = jnp.dot(qb, kt, preferred_element_type=F32)
        for g in range(PAGE_GROUP):
            logit_scr[c * FUSED_PAGES + r0 + g] = lg[:, g * PAGE_SIZE:(g + 1) * PAGE_SIZE]


def _sample_logits_finish(kn_ref, logit_scr, qb_scr, p_out, l_out, c, *, dec_seq, n_pages):
    nc = n_pages // FUSED_PAGES
    rows = N_HEADS * dec_seq
    pages_per_blk = MOBA_BLOCK // PAGE_SIZE
    nblk = n_pages // pages_per_blk

    @pl.when(c == nc - 1)
    def _():
        qb = qb_scr[...]
        blk_idx = lax.broadcasted_iota(jnp.int32, (rows, nblk), 1)
        sc = jnp.zeros((rows, nblk), F32)
        for j in range(nblk):
            blk = logit_scr[j * pages_per_blk]
            for pp in range(1, pages_per_blk):
                blk = blk + logit_scr[j * pages_per_blk + pp]
            sc = jnp.where(blk_idx == j, jnp.sum(blk, axis=-1, keepdims=True), sc)
        sel = _top_k_select(sc, blk_idx >= 0, blk_idx, nblk, axis=1)
        sel_t = jnp.where(sel, 1.0, 0.0)

        kn_page = jnp.concatenate([kn_ref[...], jnp.zeros((PAGE_SIZE - dec_seq, ATTN_W), F32)], axis=0)
        ln = lax.dot_general(qb, kn_page.astype(BF16), _NT, preferred_element_type=F32)
        key_i = lax.broadcasted_iota(jnp.int32, (rows, PAGE_SIZE), 1)
        qry_i = lax.broadcasted_iota(jnp.int32, (rows, PAGE_SIZE), 0) % dec_seq
        ln = jnp.where(key_i <= qry_i, ln, -jnp.inf)

        mx = ln
        for p in range(n_pages):
            j = p // pages_per_blk
            lp = jnp.where(sel_t[:, j:j + 1] > 0.0, logit_scr[p], -jnp.inf)
            logit_scr[p] = lp
            mx = jnp.maximum(mx, lp)
        m = jnp.max(mx, axis=-1, keepdims=True)
        pn = jnp.exp(ln - m)
        p_out[n_pages] = pn.astype(BF16)
        lsum = pn
        for p in range(n_pages):
            pp = jnp.exp(logit_scr[p] - m)
            p_out[p] = pp.astype(BF16)
            lsum = lsum + pp
        l_out[...] = jnp.broadcast_to(jnp.sum(lsum, axis=-1, keepdims=True), (rows, LANES))


def _sample_values_step(p_ref, pn_ref, l_ref, vn_ref, page_buf, acc_scr, o_ref, c, *, dec_seq, n_chunks):
    @pl.when(c == 0)
    def _():
        vn_page = jnp.concatenate([vn_ref[...], jnp.zeros((PAGE_SIZE - dec_seq, ATTN_W), F32)], axis=0)
        acc_scr[...] = jnp.dot(pn_ref[...], vn_page.astype(BF16), preferred_element_type=F32)

    acc = acc_scr[...]
    for r0 in range(0, FUSED_PAGES, VALUE_GROUP):
        vt = jnp.concatenate([page_buf[r0 + g].reshape(ATTN_W, PAGE_SIZE).astype(BF16)
                              for g in range(VALUE_GROUP)], axis=1)
        pw = jnp.concatenate([p_ref[r0 + g] for g in range(VALUE_GROUP)], axis=1)
        acc = acc + lax.dot_general(pw, vt, _NT, preferred_element_type=F32)
    acc_scr[...] = acc

    @pl.when(c == n_chunks - 1)
    def _():
        res = acc / jnp.concatenate([l_ref[...]] * (ATTN_W // LANES), axis=1)
        lane_h = lax.broadcasted_iota(jnp.int32, (dec_seq, ATTN_W), 1) // HEAD_DIM
        out = jnp.zeros((dec_seq, ATTN_W), F32)
        for h in range(N_HEADS):
            out = out + jnp.where(lane_h == h, res[h * dec_seq:(h + 1) * dec_seq, :], 0.0)
        o_ref[...] = out


def _merge_sample_kernel(a_ref, sa_ref, g0_ref, part_ref, x_ref, wb0_ref, wout_ref, gf_ref, o_ref, *, final):
    xn = _merge_out(a_ref[...], sa_ref[...], g0_ref[...], part_ref[...], x_ref[...], wb0_ref[...],
                    wout_ref[...])
    if final:
        xn = _rms_norm(xn, gf_ref[...])
    o_ref[...] = xn


def _merge_sample(attn, sa, g0, part, x2d, w_br_b, w_out_b, gf, layer, final):
    n = x2d.shape[0]
    full = lambda w: _const_spec((n, w))
    return pl.pallas_call(
        functools.partial(_merge_sample_kernel, final=final),
        grid=(1,),
        in_specs=[full(ATTN_W), full(ATTN_W), full(D_MODEL), full(D_MODEL), full(D_MODEL),
                  pl.BlockSpec((None, None, ATTN_W, D_MODEL), lambda i: (layer, 0, 0, 0)),
                  pl.BlockSpec((None, D_MODEL, D_MODEL), lambda i: (layer, 0, 0)), _const_spec((1, D_MODEL))],
        out_specs=full(D_MODEL),
        out_shape=jax.ShapeDtypeStruct((n, D_MODEL), F32),
        compiler_params=pltpu.CompilerParams(dimension_semantics=("arbitrary",),
                                             vmem_limit_bytes=VMEM_LIMIT),
        name="merge_sample",
    )(attn, sa, g0, part, x2d, w_br_b, w_out_b, gf)


def kernel(x_prompt, x_sample, cache_k, cache_v, cache_mem_k, cache_mem_v, state_conv, page_table, mem_prompt,
           g_norm, w_in, conv_w, conv_b, ln_g, ln_b, w_mem_k, w_mem_v, w_branch, w_out, g_final):
    batch, seq, _ = x_prompt.shape
    dec_batch, dec_seq, _ = x_sample.shape
    depth = w_in.shape[0]
    past_len = page_table.shape[1] * PAGE_SIZE
    assert seq % ROW_TILE == 0 and ROW_TILE == MOBA_BLOCK
    assert dec_batch * dec_seq == ROW_TILE and dec_seq == 8 and past_len % MOBA_BLOCK == 0
    assert page_table.shape[1] % FUSED_PAGES == 0

    w_in_b = w_in.astype(BF16)
    w_br_b = w_branch.astype(BF16)
    w_out_b = w_out.astype(BF16)
    gf = g_final.reshape(1, D_MODEL)

    mem_k_p, mem_v_p = _mem_proj(mem_prompt.reshape(batch * MEM_LEN, D_MODEL), w_mem_k, w_mem_v)

    tabs_p = _rope_tables(jnp.arange(seq, dtype=jnp.int32))
    tabs_s = tuple(jnp.tile(t, (dec_batch, 1))
                   for t in _rope_tables(past_len + jnp.arange(dec_seq, dtype=jnp.int32)))

    cache_kt = cache_k.transpose(0, 1, 3, 4, 2)
    cache_vt = cache_v.transpose(0, 1, 3, 4, 2)

    xp = x_prompt.reshape(batch * seq, D_MODEL)
    xs = x_sample.reshape(dec_batch * dec_seq, D_MODEL)
    kv_p = None
    cp_l, ks_l, vs_l, cs_l = [], [], [], []
    for l in range(depth):
        g = g_norm[l].reshape(1, D_MODEL)
        cw, cb = conv_w[l], conv_b[l].reshape(1, CONV_CH)
        lng, lnb = ln_g[l].reshape(1, CONV_CH), ln_b[l].reshape(1, CONV_CH)
        final = l == depth - 1

        qs, k_s, v_s, sas, g0s, parts, csts = _proj_sample(
            xs, g, w_in_b, tabs_s, cw, cb, lng, lnb, state_conv, cache_mem_k, cache_mem_v, w_br_b,
            l, dec_batch, dec_seq)
        q, kt, vt, kb, vtb, km, sa, g0, part, cst, p_s, l_s = _proj_prompt(
            page_table, xp, g, w_in_b, tabs_p, cw, cb, lng, lnb, mem_k_p, mem_v_p, w_br_b, qs, k_s, cache_kt,
            kv_p, l, batch, seq, dec_seq)
        xp, attn_s = _attn_prompt(
            page_table, q, kb, vtb, km.reshape(batch, seq // MOBA_BLOCK, ATTN_W), sa, g0, part, xp,
            w_br_b, w_out_b, gf, p_s, l_s, v_s, cache_vt, l, batch, seq, dec_seq, final)
        kv_p = (kt, vt)
        cp_l.append(cst)
        xs = _merge_sample(attn_s, sas, g0s, parts, xs, w_br_b, w_out_b, gf, l, final)
        ks_l.append(k_s); vs_l.append(v_s); cs_l.append(csts)

    y_prompt = xp.reshape(batch, seq, D_MODEL)
    y_sample = xs.reshape(dec_batch, dec_seq, D_MODEL)
    head_p = lambda t: t.reshape(depth, batch, N_HEADS, HEAD_DIM, seq).transpose(0, 1, 4, 2, 3)
    head_s = lambda ts: jnp.stack(ts).reshape(depth, dec_batch, dec_seq, N_HEADS, HEAD_DIM)
    mem_shape = (depth, batch, MEM_LEN, MEM_HEADS, MEM_HEAD_DIM)
    return (y_prompt, y_sample, head_p(kv_p[0]), head_p(kv_p[1]), jnp.stack(cp_l),
            mem_k_p.reshape(mem_shape), mem_v_p.reshape(mem_shape),
            head_s(ks_l), head_s(vs_l), jnp.stack(cs_l))
```

```python
import functools

import jax
import jax.numpy as jnp
from jax import lax
from jax.experimental import pallas as pl
from jax.experimental.pallas import tpu as pltpu

F32 = jnp.float32
BF16 = jnp.bfloat16

D_MODEL = 1024
N_HEADS = 8
HEAD_DIM = 64
ATTN_W = N_HEADS * HEAD_DIM
ROT_DIM = HEAD_DIM // 4
ROPE_THETA = 500000.0
MOBA_BLOCK = 256
MOBA_TOP_K = 3
CONV_CH = 512
CONV_K = 31
CONV_HALO = CONV_K - 1
MEM_LEN = 256
MEM_HEADS = 4
MEM_HEAD_DIM = 128
MEM_W = MEM_HEADS * MEM_HEAD_DIM
PAGE_SIZE = 128
RMS_EPS = 1e-6
LN_EPS = 1e-5
ATTN_SCALE = HEAD_DIM ** -0.5
MEM_SCALE = MEM_HEAD_DIM ** -0.5

LANES = 128
SUBLANES = 8
LOG2E = 1.4426950408889634
MASK_BIAS = -1e30
MAX_FLOOR = -1e29
HEAD_ROWS = HEAD_DIM + 16
ROW_TILE = 256
LOGITS_AHEAD = 3
FUSED_PAGES = 32
PAGE_GROUP = 2
VALUE_GROUP = 8
SEQS_PER_STEP = 4
HALO_PAD = 32
VMEM_LIMIT = 56 * 1024 * 1024

_Q, _K, _V, _AG, _CV, _CG, _CGATE, _MQ, _MG, _MERGE = (
    0, 512, 1024, 1536, 2048, 2560, 3072, 3584, 4096, 4608)
N_IN = _MERGE + 3 * D_MODEL

_NT = (((1,), (1,)), ((), ()))


def _rms_norm(x, g):
    return x * lax.rsqrt(jnp.mean(x * x, axis=-1, keepdims=True) + RMS_EPS) * g


def _layer_norm(x, g, b):
    mu = jnp.mean(x, axis=-1, keepdims=True)
    xc = x - mu
    var = jnp.mean(xc * xc, axis=-1, keepdims=True)
    return xc * lax.rsqrt(var + LN_EPS) * g + b


def _rope(xc, cosf, sina, sinb):
    return (xc * cosf + pltpu.roll(xc, LANES - ROT_DIM // 2, 1) * sina
            + pltpu.roll(xc, ROT_DIM // 2, 1) * sinb)


def _rope_tables(pos):
    half = ROT_DIM // 2
    inv = ROPE_THETA ** (-jnp.arange(0, ROT_DIM, 2, dtype=F32) / ROT_DIM)
    ang = pos.astype(F32)[:, None] * inv[None, :]
    cos, sin = jnp.cos(ang), jnp.sin(ang)
    n = pos.shape[0]
    zeros_h = jnp.zeros((n, half), F32)
    rest0 = jnp.zeros((n, HEAD_DIM - ROT_DIM), F32)
    cosf = jnp.concatenate([cos, cos, jnp.ones((n, HEAD_DIM - ROT_DIM), F32)], axis=1)
    sina = jnp.concatenate([-sin, zeros_h, rest0], axis=1)
    sinb = jnp.concatenate([zeros_h, sin, rest0], axis=1)
    rep = LANES // HEAD_DIM
    return tuple(jnp.tile(t, (1, rep)) for t in (cosf, sina, sinb))


def _top_k_select(sc, valid, idx, n, axis=0):
    scm = jnp.where(valid, sc, -jnp.inf)
    rank = jnp.zeros(sc.shape, F32)
    for j in range(n):
        sj = scm[j:j + 1, :] if axis == 0 else scm[:, j:j + 1]
        beats = (sj > scm) | ((sj == scm) & (j < idx))
        rank = rank + jnp.where(beats, 1.0, 0.0)
    return valid & (rank < float(MOBA_TOP_K))


def _block_lane(h):
    return HEAD_DIM if h % 2 == 0 else 0


def _split_bf16(x):
    hi = x.astype(BF16)
    return hi, (x - hi.astype(F32)).astype(BF16)


def _merge_out(attn, sa, g0, part, x, wb0, wout):
    a = (attn * sa).astype(BF16)
    pa = jnp.dot(a, wb0, preferred_element_type=F32)
    mix = (g0 * pa + part).astype(BF16)
    return x + jnp.dot(mix, wout, preferred_element_type=F32)


def _mem_proj_kernel(mem_ref, wk_ref, wv_ref, mk_out, mv_out):
    mb = mem_ref[...].astype(BF16)
    mk_out[...] = jnp.dot(mb, wk_ref[...].astype(BF16), preferred_element_type=F32)
    mv_out[...] = jnp.dot(mb, wv_ref[...].astype(BF16), preferred_element_type=F32)


def _mem_proj(mem2d, w_mem_k, w_mem_v):
    depth = w_mem_k.shape[0]
    rows = mem2d.shape[0]
    w_spec = pl.BlockSpec((None, D_MODEL, MEM_W), lambda l: (l, 0, 0))
    o_spec = pl.BlockSpec((None, rows, MEM_W), lambda l: (l, 0, 0))
    return pl.pallas_call(
        _mem_proj_kernel,
        grid=(depth,),
        in_specs=[pl.BlockSpec((rows, D_MODEL), lambda l: (0, 0)), w_spec, w_spec],
        out_specs=[o_spec, o_spec],
        out_shape=[jax.ShapeDtypeStruct((depth, rows, MEM_W), F32)] * 2,
        compiler_params=pltpu.CompilerParams(dimension_semantics=("arbitrary",),
                                             vmem_limit_bytes=VMEM_LIMIT),
        name="mem_proj",
    )(mem2d, w_mem_k, w_mem_v)


def _proj_prompt_kernel(pt_ref, x_ref, g_ref, w_ref, cos_ref, sina_ref, sinb_ref, cw_ref, cb_ref, lng_ref, lnb_ref,
                        mk_ref, mv_ref, wb1_ref, wb2_ref, qs_ref, kn_ref, cache_ref, *refs,
                        tiles_per_seq, n_prev, layer, dec_seq, n_pages):
    if n_prev:
        ktp_ref, vtp_ref = refs[:2]
        refs = refs[2:]
    (q_out, kt_out, vt_out, kb_out, vtb_out, km_out, sa_out, g0_out, part_out, cst_out, p_out, l_out,
     ubuf, urot, page_buf, page_sem, logit_scr, qb_scr) = refs
    tm = ROW_TILE
    step = pl.program_id(0)
    tin = step % tiles_per_seq
    steps_per_seq = n_pages // FUSED_PAGES
    sb, chunk = step // steps_per_seq, step % steps_per_seq
    _page_copies(cache_ref, pt_ref, page_buf, page_sem, layer, sb, chunk, start=True)
    if n_prev:
        kt_out[0:n_prev] = ktp_ref[...]
        vt_out[0:n_prev] = vtp_ref[...]
    hb = _rms_norm(x_ref[...], g_ref[...]).astype(BF16)

    def seg(a, width):
        return jnp.dot(hb, w_ref[:, a:a + width], preferred_element_type=F32)

    cosf, sina, sinb = cos_ref[...], sina_ref[...], sinb_ref[...]
    zq = seg(_Q, ATTN_W)
    zk = seg(_K, ATTN_W)
    for c in range(ATTN_W // LANES):
        sl = slice(c * LANES, (c + 1) * LANES)
        q_out[:, sl] = _rope(zq[:, sl], cosf, sina, sinb)
        kr = _rope(zk[:, sl], cosf, sina, sinb)
        kt_out[n_prev, sl, :] = kr.T
        lane = lax.broadcasted_iota(jnp.int32, (tm, LANES), 1)
        for hh in range(2):
            onehot = jnp.where(lane == _block_lane(2 * c + hh) + tin, 1.0, 0.0)
            kb_out[:, (2 * c + hh) * LANES:(2 * c + hh + 1) * LANES] = jnp.where(
                lane // HEAD_DIM == hh, kr, onehot).astype(BF16)
        km_out[:, sl] = jnp.mean(kr, axis=0, keepdims=True)
    zvt = seg(_V, ATTN_W).T
    vt_out[n_prev] = zvt
    for h in range(N_HEADS):
        vtb_out[h * HEAD_ROWS:h * HEAD_ROWS + HEAD_DIM, :] = zvt[h * HEAD_DIM:(h + 1) * HEAD_DIM, :].astype(BF16)
        vtb_out[h * HEAD_ROWS + HEAD_DIM:(h + 1) * HEAD_ROWS, :] = jnp.ones((HEAD_ROWS - HEAD_DIM, tm), BF16)
    sa_out[...] = jax.nn.silu(seg(_AG, ATTN_W))

    u = seg(_CV, CONV_CH) * jax.nn.sigmoid(seg(_CG, CONV_CH))

    @pl.when(tin == 0)
    def _():
        ubuf[0:HALO_PAD, :] = jnp.zeros((HALO_PAD, CONV_CH), F32)

    @pl.when(tin != 0)
    def _():
        ubuf[HALO_PAD - CONV_HALO:HALO_PAD, :] = ubuf[tm + HALO_PAD - CONV_HALO:tm + HALO_PAD, :]

    ubuf[HALO_PAD:HALO_PAD + tm, :] = u
    nrot = tm + HALO_PAD - SUBLANES
    for r in range(1, SUBLANES):
        urot[r - 1] = ubuf[r:r + nrot, :]
    cst_out[...] = ubuf[tm + HALO_PAD - CONV_HALO:tm + HALO_PAD, :]
    conv = jnp.broadcast_to(cb_ref[...], (tm, CONV_CH))
    for t in range(CONV_K):
        a, r = divmod(HALO_PAD - CONV_HALO + t, SUBLANES)
        rows = slice(a * SUBLANES, a * SUBLANES + tm)
        conv = conv + cw_ref[t:t + 1, :] * (ubuf[rows, :] if r == 0 else urot[r - 1, rows, :])
    cbr = jax.nn.silu(_layer_norm(conv, lng_ref[...], lnb_ref[...])) * jax.nn.silu(seg(_CGATE, CONV_CH))

    zmq = seg(_MQ, MEM_W)
    mparts = []
    for hd in range(MEM_HEADS):
        sl = slice(hd * MEM_HEAD_DIM, (hd + 1) * MEM_HEAD_DIM)
        s = lax.dot_general(zmq[:, sl].astype(BF16), mk_ref[:, sl].astype(BF16), _NT,
                            preferred_element_type=F32) * MEM_SCALE
        p = jnp.exp(s - jnp.max(s, axis=-1, keepdims=True))
        o = jnp.dot(p.astype(BF16), mv_ref[:, sl].astype(BF16), preferred_element_type=F32)
        mparts.append(o / jnp.sum(p, axis=-1, keepdims=True))
    mbr = jnp.concatenate(mparts, axis=1) * jax.nn.silu(seg(_MG, MEM_W))

    pc = jnp.dot(cbr.astype(BF16), wb1_ref[...], preferred_element_type=F32)
    pm = jnp.dot(mbr.astype(BF16), wb2_ref[...], preferred_element_type=F32)
    g0_out[...] = jax.nn.sigmoid(seg(_MERGE, D_MODEL))
    part_out[...] = (jax.nn.sigmoid(seg(_MERGE + D_MODEL, D_MODEL)) * pc
                     + jax.nn.sigmoid(seg(_MERGE + 2 * D_MODEL, D_MODEL)) * pm)

    _page_copies(cache_ref, pt_ref, page_buf, page_sem, layer, sb, chunk, start=False)
    _sample_logits_pages(qs_ref, page_buf, logit_scr, qb_scr, chunk, dec_seq=dec_seq)
    _sample_logits_finish(kn_ref, logit_scr, qb_scr, p_out, l_out, chunk, dec_seq=dec_seq, n_pages=n_pages)


def _const_spec(shape, ngrid=1):
    zeros = (0,) * len(shape)
    if ngrid == 1:
        return pl.BlockSpec(shape, lambda i: zeros)
    return pl.BlockSpec(shape, lambda i, j: zeros)


def _proj_prompt(page_table, x2d, g, w_in_b, tabs, cw, cb, lng, lnb, mk, mv, w_br_b, q_s, k_s, cache_kt,
                 kv_prev, layer, batch, seq, dec_seq):
    tm = ROW_TILE
    n = batch * seq
    tps = seq // tm
    nt = n // tm
    dec_batch, n_pages = page_table.shape
    sps = n_pages // FUSED_PAGES
    assert nt == dec_batch * sps
    rows_s = N_HEADS * dec_seq
    n_prev = 0 if kv_prev is None else kv_prev[0].shape[0]
    const = lambda shape: pl.BlockSpec(shape, lambda t, pt: (0,) * len(shape))
    row = lambda w: pl.BlockSpec((tm, w), lambda t, pt: (t, 0))
    tab = pl.BlockSpec((tm, LANES), lambda t, pt: (t % tps, 0))
    mem = pl.BlockSpec((None, MEM_LEN, MEM_W), lambda t, pt: (layer, t // tps, 0))
    wbr = lambda br: pl.BlockSpec((None, None, ATTN_W, D_MODEL), lambda t, pt: (layer, br, 0, 0))
    seq_rows = pl.BlockSpec((dec_seq, ATTN_W), lambda t, pt: (t // sps, 0))
    in_specs = [
        row(D_MODEL), const((1, D_MODEL)),
        pl.BlockSpec((None, D_MODEL, N_IN), lambda t, pt: (layer, 0, 0), pipeline_mode=pl.Buffered(1)),
        tab, tab, tab,
        const((CONV_K, CONV_CH)), const((1, CONV_CH)), const((1, CONV_CH)), const((1, CONV_CH)),
        mem, mem, wbr(1), wbr(2),
        seq_rows, seq_rows, pl.BlockSpec(memory_space=pl.ANY),
    ]
    args = [page_table, x2d, g, w_in_b, *tabs, cw, cb, lng, lnb, mk, mv, w_br_b, w_br_b, q_s, k_s, cache_kt]
    if n_prev:
        prev = pl.BlockSpec((n_prev, None, ATTN_W, tm), lambda t, pt: (0, t // tps, 0, t % tps))
        in_specs += [prev, prev]
        args += list(kv_prev)
    out_shape = [
        jax.ShapeDtypeStruct((n, ATTN_W), F32),
        jax.ShapeDtypeStruct((n_prev + 1, batch, ATTN_W, seq), F32),
        jax.ShapeDtypeStruct((n_prev + 1, batch, ATTN_W, seq), F32),
        jax.ShapeDtypeStruct((nt, tm, N_HEADS * LANES), BF16),
        jax.ShapeDtypeStruct((nt, N_HEADS * HEAD_ROWS, tm), BF16),
        jax.ShapeDtypeStruct((nt, 1, ATTN_W), F32),
        jax.ShapeDtypeStruct((n, ATTN_W), F32),
        jax.ShapeDtypeStruct((n, D_MODEL), F32),
        jax.ShapeDtypeStruct((n, D_MODEL), F32),
        jax.ShapeDtypeStruct((batch, CONV_HALO, CONV_CH), F32),
        jax.ShapeDtypeStruct((dec_batch, n_pages + 1, rows_s, PAGE_SIZE), BF16),
        jax.ShapeDtypeStruct((dec_batch, rows_s, LANES), F32),
    ]
    seq_t = pl.BlockSpec((n_prev + 1, None, ATTN_W, tm), lambda t, pt: (0, t // tps, 0, t % tps))
    out_specs = [
        row(ATTN_W), seq_t, seq_t,
        pl.BlockSpec((None, tm, N_HEADS * LANES), lambda t, pt: (t, 0, 0)),
        pl.BlockSpec((None, N_HEADS * HEAD_ROWS, tm), lambda t, pt: (t, 0, 0)),
        pl.BlockSpec((None, 1, ATTN_W), lambda t, pt: (t, 0, 0)),
        row(ATTN_W), row(D_MODEL), row(D_MODEL),
        pl.BlockSpec((None, CONV_HALO, CONV_CH), lambda t, pt: (t // tps, 0, 0)),
        pl.BlockSpec((None, n_pages + 1, rows_s, PAGE_SIZE), lambda t, pt: (t // sps, 0, 0, 0)),
        pl.BlockSpec((None, rows_s, LANES), lambda t, pt: (t // sps, 0, 0)),
    ]
    return pl.pallas_call(
        functools.partial(_proj_prompt_kernel, tiles_per_seq=tps, n_prev=n_prev, layer=layer,
                          dec_seq=dec_seq, n_pages=n_pages),
        grid_spec=pltpu.PrefetchScalarGridSpec(
            num_scalar_prefetch=1,
            grid=(nt,),
            in_specs=in_specs,
            out_specs=out_specs,
            scratch_shapes=[pltpu.VMEM((tm + HALO_PAD, CONV_CH), F32),
                            pltpu.VMEM((SUBLANES - 1, tm + HALO_PAD - SUBLANES, CONV_CH), F32),
                            pltpu.VMEM((FUSED_PAGES, N_HEADS, HEAD_DIM, PAGE_SIZE), F32),
                            pltpu.SemaphoreType.DMA((FUSED_PAGES,)),
                            pltpu.VMEM((n_pages, rows_s, PAGE_SIZE), F32),
                            pltpu.VMEM((rows_s, ATTN_W), BF16)]),
        out_shape=out_shape,
        compiler_params=pltpu.CompilerParams(dimension_semantics=("arbitrary",),
                                             vmem_limit_bytes=VMEM_LIMIT),
        name="proj_prompt",
    )(*args)


def _attn_prompt_kernel(pt_ref, q_ref, kb_ref, vt_ref, km_ref, sa_ref, g0_ref, part_ref, x_ref, wb0_ref, wout_ref,
                        gf_ref, ps_ref, pns_ref, ls_ref, vn_ref, cache_ref, o_ref, so_ref,
                        sc_scr, sel_scr, qb_scr, m_scr, acc_scr, page_buf, page_sem, sacc_scr,
                        *, final, nblk, layer, dec_seq, n_pages):
    tq = ROW_TILE
    i = pl.program_id(1)
    step = pl.program_id(0) * nblk + i
    steps_per_seq = n_pages // FUSED_PAGES
    sb, chunk = step // steps_per_seq, step % steps_per_seq
    _page_copies(cache_ref, pt_ref, page_buf, page_sem, layer, sb, chunk, start=True)
    blk_idx = lax.broadcasted_iota(jnp.int32, (nblk, tq), 0)
    valid = blk_idx < i
    lane = lax.broadcasted_iota(jnp.int32, (tq, LANES), 1)
    causal = (lax.broadcasted_iota(jnp.int32, (MOBA_BLOCK, tq), 0)
              <= lax.broadcasted_iota(jnp.int32, (MOBA_BLOCK, tq), 1))
    pair = lambda h: slice((h // 2) * LANES, (h // 2 + 1) * LANES)
    col = lambda h: slice(h * LANES, (h + 1) * LANES)
    rows = lambda h: slice(h * HEAD_ROWS, (h + 1) * HEAD_ROWS)
    bcast = lambda r: jnp.broadcast_to(r, (SUBLANES, tq))
    own_half = lambda h: jnp.where((lane // HEAD_DIM) == h % 2, q_ref[:, pair(h)], 0.0)

    for h in range(N_HEADS):
        qm = own_half(h)
        q_hi, q_lo = _split_bf16(qm)
        km_hi, km_lo = _split_bf16(km_ref[:, pair(h)])
        sc = lax.dot_general(jnp.concatenate([km_hi, km_hi, km_lo], axis=1),
                             jnp.concatenate([q_hi, q_lo, q_hi], axis=1), _NT,
                             preferred_element_type=F32)
        sc_scr[h] = jnp.where(valid, sc, -jnp.inf)
        sel_scr[h] = jnp.zeros((nblk, tq), F32)
        m_scr[h] = jnp.full((SUBLANES, tq), MAX_FLOOR, F32)
    acc_scr[...] = jnp.zeros((N_HEADS * HEAD_ROWS, tq), F32)

    def rank_body(j, carry):
        for h in range(N_HEADS):
            scm = sc_scr[h]
            sj = sc_scr[h, pl.ds(j, 1), :]
            before = (sj > scm) | ((sj == scm) & (j < blk_idx))
            sel_scr[h] = sel_scr[h] + jnp.where(before, 1.0, 0.0)
        return carry

    lax.fori_loop(0, i, rank_body, 0)
    for h in range(N_HEADS):
        keep = (valid & (sel_scr[h] < float(MOBA_TOP_K))) | (blk_idx == i)
        pieces = [jnp.where(keep, 0.0, MASK_BIAS)]
        if _block_lane(h):
            pieces.insert(0, jnp.zeros((_block_lane(h), tq), F32))
        pieces.append(jnp.zeros((LANES - _block_lane(h) - nblk, tq), F32))
        bias = jnp.concatenate(pieces, axis=0).T
        qb_scr[h] = (own_half(h) * (ATTN_SCALE * LOG2E) + bias).astype(BF16)

    def logits(j, h):
        return lax.dot_general(kb_ref[j, :, col(h)], qb_scr[h], _NT, preferred_element_type=F32)

    def softmax_step(j, h, s):
        m_old = m_scr[h, 0:1, :]
        m_new = jnp.maximum(m_old, jnp.max(s, axis=0, keepdims=True))
        alpha = jnp.exp2(m_old - m_new)
        p = jnp.exp2(s - m_new)
        m_scr[h] = bcast(m_new)
        acc_scr[rows(h), :] = alpha * acc_scr[rows(h), :] + jnp.dot(
            vt_ref[j, rows(h), :], p.astype(BF16), preferred_element_type=F32)

    def body(j, ahead):
        ahead = list(ahead)
        for h in range(N_HEADS):
            s = ahead.pop(0)
            nh = h + LOGITS_AHEAD
            ahead.append(logits(j, nh) if nh < N_HEADS else logits(j + 1, nh - N_HEADS))
            softmax_step(j, h, s)
        return tuple(ahead)

    ahead = list(lax.fori_loop(0, i, body, tuple(logits(0, h) for h in range(LOGITS_AHEAD))))
    for h in range(N_HEADS):
        s = ahead.pop(0)
        if h + LOGITS_AHEAD < N_HEADS:
            ahead.append(logits(i, h + LOGITS_AHEAD))
        softmax_step(i, h, jnp.where(causal, s, -jnp.inf))
    outs = []
    for h in range(N_HEADS):
        blk = acc_scr[rows(h), :]
        outs.append(blk[0:HEAD_DIM, :] / blk[HEAD_DIM:HEAD_DIM + 1, :])

    attn = jnp.concatenate(outs, axis=0).T
    xn = _merge_out(attn, sa_ref[...], g0_ref[...], part_ref[...], x_ref[...], wb0_ref[...], wout_ref[...])
    if final:
        xn = _rms_norm(xn, gf_ref[...])
    o_ref[...] = xn

    _page_copies(cache_ref, pt_ref, page_buf, page_sem, layer, sb, chunk, start=False)
    _sample_values_step(ps_ref, pns_ref, ls_ref, vn_ref, page_buf, sacc_scr, so_ref, chunk,
                        dec_seq=dec_seq, n_chunks=steps_per_seq)


def _attn_prompt(page_table, q, kb, vt, km, sa, g0, part, x2d, w_br_b, w_out_b, gf, p_s, l_s, v_s, cache_vt,
                 layer, batch, seq, dec_seq, final):
    tq = ROW_TILE
    nblk = seq // MOBA_BLOCK
    dec_batch, n_pages = page_table.shape
    sps = n_pages // FUSED_PAGES
    assert batch * nblk == dec_batch * sps
    rows_s = N_HEADS * dec_seq
    sstep = lambda b, i: (b * nblk + i) // sps
    row = lambda w: pl.BlockSpec((tq, w), lambda b, i, pt: (b * nblk + i, 0))
    seq_rows = pl.BlockSpec((dec_seq, ATTN_W), lambda b, i, pt: (sstep(b, i), 0))
    in_specs = [
        row(ATTN_W),
        pl.BlockSpec((nblk, MOBA_BLOCK, N_HEADS * LANES), lambda b, i, pt: (b, 0, 0)),
        pl.BlockSpec((nblk, N_HEADS * HEAD_ROWS, MOBA_BLOCK), lambda b, i, pt: (b, 0, 0)),
        pl.BlockSpec((None, nblk, ATTN_W), lambda b, i, pt: (b, 0, 0)),
        row(ATTN_W), row(D_MODEL), row(D_MODEL), row(D_MODEL),
        pl.BlockSpec((None, None, ATTN_W, D_MODEL), lambda b, i, pt: (layer, 0, 0, 0)),
        pl.BlockSpec((None, D_MODEL, D_MODEL), lambda b, i, pt: (layer, 0, 0)),
        pl.BlockSpec((1, D_MODEL), lambda b, i, pt: (0, 0)),
        pl.BlockSpec((None, FUSED_PAGES, rows_s, PAGE_SIZE),
                     lambda b, i, pt: (sstep(b, i), (b * nblk + i) % sps, 0, 0)),
        pl.BlockSpec((None, None, rows_s, PAGE_SIZE), lambda b, i, pt: (sstep(b, i), n_pages, 0, 0)),
        pl.BlockSpec((None, rows_s, LANES), lambda b, i, pt: (sstep(b, i), 0, 0)),
        seq_rows, pl.BlockSpec(memory_space=pl.ANY),
    ]
    return pl.pallas_call(
        functools.partial(_attn_prompt_kernel, final=final, nblk=nblk, layer=layer, dec_seq=dec_seq,
                          n_pages=n_pages),
        grid_spec=pltpu.PrefetchScalarGridSpec(
            num_scalar_prefetch=1,
            grid=(batch, nblk),
            in_specs=in_specs,
            out_specs=[row(D_MODEL), seq_rows],
            scratch_shapes=[pltpu.VMEM((N_HEADS, nblk, tq), F32),
                            pltpu.VMEM((N_HEADS, nblk, tq), F32),
                            pltpu.VMEM((N_HEADS, tq, LANES), BF16),
                            pltpu.VMEM((N_HEADS, SUBLANES, tq), F32),
                            pltpu.VMEM((N_HEADS * HEAD_ROWS, tq), F32),
                            pltpu.VMEM((FUSED_PAGES, N_HEADS, HEAD_DIM, PAGE_SIZE), F32),
                            pltpu.SemaphoreType.DMA((FUSED_PAGES,)),
                            pltpu.VMEM((rows_s, ATTN_W), F32)]),
        out_shape=[jax.ShapeDtypeStruct((batch * seq, D_MODEL), F32),
                   jax.ShapeDtypeStruct((dec_batch * dec_seq, ATTN_W), F32)],
        compiler_params=pltpu.CompilerParams(dimension_semantics=("arbitrary", "arbitrary"),
                                             vmem_limit_bytes=VMEM_LIMIT),
        name="attn_prompt",
    )(page_table, q, kb, vt, km, sa, g0, part, x2d, w_br_b, w_out_b, gf, p_s, p_s, l_s, v_s, cache_vt)


def _proj_sample_kernel(x_ref, g_ref, w_ref, cos_ref, sina_ref, sinb_ref, cw_ref, cb_ref, lng_ref, lnb_ref,
                        st_ref, mk_ref, mv_ref, wb1_ref, wb2_ref,
                        q_out, k_out, v_out, sa_out, g0_out, part_out, cst_out,
                        u_scr, cgate_scr, mq_scr, mgate_scr, g12_scr, c_scr, m_scr, full_scr, *, dec_seq):
    b = pl.program_id(0)
    nb = pl.num_programs(0)

    @pl.when(b == 0)
    def _():
        hb = _rms_norm(x_ref[...], g_ref[...]).astype(BF16)

        def seg(a, width):
            return jnp.dot(hb, w_ref[:, a:a + width], preferred_element_type=F32)

        cosf, sina, sinb = cos_ref[...], sina_ref[...], sinb_ref[...]
        zq = seg(_Q, ATTN_W)
        zk = seg(_K, ATTN_W)
        for c in range(ATTN_W // LANES):
            sl = slice(c * LANES, (c + 1) * LANES)
            q_out[:, sl] = _rope(zq[:, sl], cosf, sina, sinb)
            k_out[:, sl] = _rope(zk[:, sl], cosf, sina, sinb)
        v_out[...] = seg(_V, ATTN_W)
        sa_out[...] = jax.nn.silu(seg(_AG, ATTN_W))
        u_scr[...] = seg(_CV, CONV_CH) * jax.nn.sigmoid(seg(_CG, CONV_CH))
        cgate_scr[...] = jax.nn.silu(seg(_CGATE, CONV_CH))
        mq_scr[...] = seg(_MQ, MEM_W)
        mgate_scr[...] = jax.nn.silu(seg(_MG, MEM_W))
        g0_out[...] = jax.nn.sigmoid(seg(_MERGE, D_MODEL))
        g12_scr[:, 0:D_MODEL] = jax.nn.sigmoid(seg(_MERGE + D_MODEL, D_MODEL))
        g12_scr[:, D_MODEL:2 * D_MODEL] = jax.nn.sigmoid(seg(_MERGE + 2 * D_MODEL, D_MODEL))

    for sq in range(SEQS_PER_STEP):
        r0 = pl.multiple_of((b * SEQS_PER_STEP + sq) * dec_seq, dec_seq)
        rows = pl.ds(r0, dec_seq)

        full_scr[sq, 0:CONV_HALO, :] = st_ref[sq]
        full_scr[sq, CONV_HALO:CONV_HALO + dec_seq, :] = u_scr[rows, :]
        conv = jnp.broadcast_to(cb_ref[...], (dec_seq, CONV_CH))
        for t in range(CONV_K):
            conv = conv + cw_ref[t:t + 1, :] * full_scr[sq, t:t + dec_seq, :]
        cst_out[sq] = full_scr[sq, dec_seq:dec_seq + CONV_HALO, :]
        c_scr[rows, :] = jax.nn.silu(_layer_norm(conv, lng_ref[...], lnb_ref[...])) * cgate_scr[rows, :]

        mq = mq_scr[rows, :]
        mparts = []
        for hd in range(MEM_HEADS):
            sl = slice(hd * MEM_HEAD_DIM, (hd + 1) * MEM_HEAD_DIM)
            s = lax.dot_general(mq[:, sl], mk_ref[sq, :, hd, :], _NT, preferred_element_type=F32) * MEM_SCALE
            p = jnp.exp(s - jnp.max(s, axis=-1, keepdims=True))
            o = jnp.dot(p, mv_ref[sq, :, hd, :], preferred_element_type=F32)
            mparts.append(o / jnp.sum(p, axis=-1, keepdims=True))
        m_scr[rows, :] = jnp.concatenate(mparts, axis=1) * mgate_scr[rows, :]

    @pl.when(b == nb - 1)
    def _():
        pc = jnp.dot(c_scr[...].astype(BF16), wb1_ref[...], preferred_element_type=F32)
        pm = jnp.dot(m_scr[...].astype(BF16), wb2_ref[...], preferred_element_type=F32)
        part_out[...] = g12_scr[:, 0:D_MODEL] * pc + g12_scr[:, D_MODEL:2 * D_MODEL] * pm


def _proj_sample(x2d, g, w_in_b, tabs, cw, cb, lng, lnb, state_conv, cache_mem_k, cache_mem_v, w_br_b,
                 layer, dec_batch, dec_seq):
    n = dec_batch * dec_seq
    full = lambda w: _const_spec((n, w))
    assert dec_batch % SEQS_PER_STEP == 0
    mem = pl.BlockSpec((None, SEQS_PER_STEP, MEM_LEN, MEM_HEADS, MEM_HEAD_DIM), lambda b: (layer, b, 0, 0, 0))
    wbr = lambda br: pl.BlockSpec((None, None, ATTN_W, D_MODEL), lambda b: (layer, br, 0, 0))
    in_specs = [
        full(D_MODEL), _const_spec((1, D_MODEL)),
        pl.BlockSpec((None, D_MODEL, N_IN), lambda b: (layer, 0, 0), pipeline_mode=pl.Buffered(1)),
        full(LANES), full(LANES), full(LANES),
        _const_spec((CONV_K, CONV_CH)), _const_spec((1, CONV_CH)), _const_spec((1, CONV_CH)),
        _const_spec((1, CONV_CH)),
        pl.BlockSpec((None, SEQS_PER_STEP, CONV_HALO, CONV_CH), lambda b: (layer, b, 0, 0)),
        mem, mem, wbr(1), wbr(2),
    ]
    out_shape = [jax.ShapeDtypeStruct((n, ATTN_W), F32)] * 4 + [
        jax.ShapeDtypeStruct((n, D_MODEL), F32), jax.ShapeDtypeStruct((n, D_MODEL), F32),
        jax.ShapeDtypeStruct((dec_batch, CONV_HALO, CONV_CH), F32)]
    out_specs = [full(ATTN_W)] * 4 + [full(D_MODEL), full(D_MODEL),
                                      pl.BlockSpec((SEQS_PER_STEP, CONV_HALO, CONV_CH), lambda b: (b, 0, 0))]
    scr = lambda w: pltpu.VMEM((n, w), F32)
    return pl.pallas_call(
        functools.partial(_proj_sample_kernel, dec_seq=dec_seq),
        grid=(dec_batch // SEQS_PER_STEP,),
        in_specs=in_specs,
        out_specs=out_specs,
        out_shape=out_shape,
        scratch_shapes=[scr(CONV_CH), scr(CONV_CH), scr(MEM_W), scr(MEM_W), scr(2 * D_MODEL),
                        scr(CONV_CH), scr(MEM_W),
                        pltpu.VMEM((SEQS_PER_STEP, CONV_HALO + dec_seq + 2, CONV_CH), F32)],
        compiler_params=pltpu.CompilerParams(dimension_semantics=("arbitrary",),
                                             vmem_limit_bytes=VMEM_LIMIT),
        name="proj_sample",
    )(x2d, g, w_in_b, *tabs, cw, cb, lng, lnb, state_conv, cache_mem_k, cache_mem_v, w_br_b, w_br_b)


def _page_copies(cache_ref, pt_ref, page_buf, page_sem, layer, sb, chunk, *, start):
    for r in range(FUSED_PAGES):
        page = pt_ref[sb, chunk * FUSED_PAGES + r]
        copy = pltpu.make_async_copy(cache_ref.at[layer, page], page_buf.at[r], page_sem.at[r])
        if start:
            copy.start()
        else:
            copy.wait()


def _head_rows(x, dec_seq):
    rows = N_HEADS * dec_seq
    tiled = jnp.concatenate([x] * N_HEADS, axis=0)
    row_h = lax.broadcasted_iota(jnp.int32, (rows, ATTN_W), 0) // dec_seq
    lane_h = lax.broadcasted_iota(jnp.int32, (rows, ATTN_W), 1) // HEAD_DIM
    return jnp.where(row_h == lane_h, tiled, 0.0)


def _sample_logits_pages(q_ref, page_buf, logit_scr, qb_scr, c, *, dec_seq):
    @pl.when(c == 0)
    def _():
        qb_scr[...] = (_head_rows(q_ref[...], dec_seq) * ATTN_SCALE).astype(BF16)

    qb = qb_scr[...]
    for r0 in range(0, FUSED_PAGES, PAGE_GROUP):
        kt = jnp.concatenate([page_buf[r0 + g].reshape(ATTN_W, PAGE_SIZE).astype(BF16)
                              for g in range(PAGE_GROUP)], axis=1)
        lg = jnp.dot(qb, kt, preferred_element_type=F32)
        for g in range(PAGE_GROUP):
            logit_scr[c * FUSED_PAGES + r0 + g] = lg[:, g * PAGE_SIZE:(g + 1) * PAGE_SIZE]


def _sample_logits_finish(kn_ref, logit_scr, qb_scr, p_out, l_out, c, *, dec_seq, n_pages):
    nc = n_pages // FUSED_PAGES
    rows = N_HEADS * dec_seq
    pages_per_blk = MOBA_BLOCK // PAGE_SIZE
    nblk = n_pages // pages_per_blk

    @pl.when(c == nc - 1)
    def _():
        qb = qb_scr[...]
        blk_idx = lax.broadcasted_iota(jnp.int32, (rows, nblk), 1)
        sc = jnp.zeros((rows, nblk), F32)
        for j in range(nblk):
            blk = logit_scr[j * pages_per_blk]
            for pp in range(1, pages_per_blk):
                blk = blk + logit_scr[j * pages_per_blk + pp]
            sc = jnp.where(blk_idx == j, jnp.sum(blk, axis=-1, keepdims=True), sc)
        sel = _top_k_select(sc, blk_idx >= 0, blk_idx, nblk, axis=1)
        sel_t = jnp.where(sel, 1.0, 0.0)

        kn_page = jnp.concatenate([kn_ref[...], jnp.zeros((PAGE_SIZE - dec_seq, ATTN_W), F32)], axis=0)
        ln = lax.dot_general(qb, kn_page.astype(BF16), _NT, preferred_element_type=F32)
        key_i = lax.broadcasted_iota(jnp.int32, (rows, PAGE_SIZE), 1)
        qry_i = lax.broadcasted_iota(jnp.int32, (rows, PAGE_SIZE), 0) % dec_seq
        ln = jnp.where(key_i <= qry_i, ln, -jnp.inf)

        mx = ln
        for p in range(n_pages):
            j = p // pages_per_blk
            lp = jnp.where(sel_t[:, j:j + 1] > 0.0, logit_scr[p], -jnp.inf)
            logit_scr[p] = lp
            mx = jnp.maximum(mx, lp)
        m = jnp.max(mx, axis=-1, keepdims=True)
        pn = jnp.exp(ln - m)
        p_out[n_pages] = pn.astype(BF16)
        lsum = pn
        for p in range(n_pages):
            pp = jnp.exp(logit_scr[p] - m)
            p_out[p] = pp.astype(BF16)
            lsum = lsum + pp
        l_out[...] = jnp.broadcast_to(jnp.sum(lsum, axis=-1, keepdims=True), (rows, LANES))


def _sample_values_step(p_ref, pn_ref, l_ref, vn_ref, page_buf, acc_scr, o_ref, c, *, dec_seq, n_chunks):
    @pl.when(c == 0)
    def _():
        vn_page = jnp.concatenate([vn_ref[...], jnp.zeros((PAGE_SIZE - dec_seq, ATTN_W), F32)], axis=0)
        acc_scr[...] = jnp.dot(pn_ref[...], vn_page.astype(BF16), preferred_element_type=F32)

    acc = acc_scr[...]
    for r0 in range(0, FUSED_PAGES, VALUE_GROUP):
        vt = jnp.concatenate([page_buf[r0 + g].reshape(ATTN_W, PAGE_SIZE).astype(BF16)
                              for g in range(VALUE_GROUP)], axis=1)
        pw = jnp.concatenate([p_ref[r0 + g] for g in range(VALUE_GROUP)], axis=1)
        acc = acc + lax.dot_general(pw, vt, _NT, preferred_element_type=F32)
    acc_scr[...] = acc

    @pl.when(c == n_chunks - 1)
    def _():
        res = acc / jnp.concatenate([l_ref[...]] * (ATTN_W // LANES), axis=1)
        lane_h = lax.broadcasted_iota(jnp.int32, (dec_seq, ATTN_W), 1) // HEAD_DIM
        out = jnp.zeros((dec_seq, ATTN_W), F32)
        for h in range(N_HEADS):
            out = out + jnp.where(lane_h == h, res[h * dec_seq:(h + 1) * dec_seq, :], 0.0)
        o_ref[...] = out


def _merge_sample_kernel(a_ref, sa_ref, g0_ref, part_ref, x_ref, wb0_ref, wout_ref, gf_ref, o_ref, *, final):
    xn = _merge_out(a_ref[...], sa_ref[...], g0_ref[...], part_ref[...], x_ref[...], wb0_ref[...],
                    wout_ref[...])
    if final:
        xn = _rms_norm(xn, gf_ref[...])
    o_ref[...] = xn


def _merge_sample(attn, sa, g0, part, x2d, w_br_b, w_out_b, gf, layer, final):
    n = x2d.shape[0]
    full = lambda w: _const_spec((n, w))
    return pl.pallas_call(
        functools.partial(_merge_sample_kernel, final=final),
        grid=(1,),
        in_specs=[full(ATTN_W), full(ATTN_W), full(D_MODEL), full(D_MODEL), full(D_MODEL),
                  pl.BlockSpec((None, None, ATTN_W, D_MODEL), lambda i: (layer, 0, 0, 0)),
                  pl.BlockSpec((None, D_MODEL, D_MODEL), lambda i: (layer, 0, 0)), _const_spec((1, D_MODEL))],
        out_specs=full(D_MODEL),
        out_shape=jax.ShapeDtypeStruct((n, D_MODEL), F32),
        compiler_params=pltpu.CompilerParams(dimension_semantics=("arbitrary",),
                                             vmem_limit_bytes=VMEM_LIMIT),
        name="merge_sample",
    )(attn, sa, g0, part, x2d, w_br_b, w_out_b, gf)


def kernel(x_prompt, x_sample, cache_k, cache_v, cache_mem_k, cache_mem_v, state_conv, page_table, mem_prompt,
           g_norm, w_in, conv_w, conv_b, ln_g, ln_b, w_mem_k, w_mem_v, w_branch, w_out, g_final):
    batch, seq, _ = x_prompt.shape
    dec_batch, dec_seq, _ = x_sample.shape
    depth = w_in.shape[0]
    past_len = page_table.shape[1] * PAGE_SIZE
    assert seq % ROW_TILE == 0 and ROW_TILE == MOBA_BLOCK
    assert dec_batch * dec_seq == ROW_TILE and dec_seq == 8 and past_len % MOBA_BLOCK == 0
    assert page_table.shape[1] % FUSED_PAGES == 0

    w_in_b = w_in.astype(BF16)
    w_br_b = w_branch.astype(BF16)
    w_out_b = w_out.astype(BF16)
    gf = g_final.reshape(1, D_MODEL)

    mem_k_p, mem_v_p = _mem_proj(mem_prompt.reshape(batch * MEM_LEN, D_MODEL), w_mem_k, w_mem_v)

    tabs_p = _rope_tables(jnp.arange(seq, dtype=jnp.int32))
    tabs_s = tuple(jnp.tile(t, (dec_batch, 1))
                   for t in _rope_tables(past_len + jnp.arange(dec_seq, dtype=jnp.int32)))

    cache_kt = cache_k.transpose(0, 1, 3, 4, 2)
    cache_vt = cache_v.transpose(0, 1, 3, 4, 2)

    xp = x_prompt.reshape(batch * seq, D_MODEL)
    xs = x_sample.reshape(dec_batch * dec_seq, D_MODEL)
    kv_p = None
    cp_l, ks_l, vs_l, cs_l = [], [], [], []
    for l in range(depth):
        g = g_norm[l].reshape(1, D_MODEL)
        cw, cb = conv_w[l], conv_b[l].reshape(1, CONV_CH)
        lng, lnb = ln_g[l].reshape(1, CONV_CH), ln_b[l].reshape(1, CONV_CH)
        final = l == depth - 1

        qs, k_s, v_s, sas, g0s, parts, csts = _proj_sample(
            xs, g, w_in_b, tabs_s, cw, cb, lng, lnb, state_conv, cache_mem_k, cache_mem_v, w_br_b,
            l, dec_batch, dec_seq)
        q, kt, vt, kb, vtb, km, sa, g0, part, cst, p_s, l_s = _proj_prompt(
            page_table, xp, g, w_in_b, tabs_p, cw, cb, lng, lnb, mem_k_p, mem_v_p, w_br_b, qs, k_s, cache_kt,
            kv_p, l, batch, seq, dec_seq)
        xp, attn_s = _attn_prompt(
            page_table, q, kb, vtb, km.reshape(batch, seq // MOBA_BLOCK, ATTN_W), sa, g0, part, xp,
            w_br_b, w_out_b, gf, p_s, l_s, v_s, cache_vt, l, batch, seq, dec_seq, final)
        kv_p = (kt, vt)
        cp_l.append(cst)
        xs = _merge_sample(attn_s, sas, g0s, parts, xs, w_br_b, w_out_b, gf, l, final)
        ks_l.append(k_s); vs_l.append(v_s); cs_l.append(csts)

    y_prompt = xp.reshape(batch, seq, D_MODEL)
    y_sample = xs.reshape(dec_batch, dec_seq, D_MODEL)
    head_p = lambda t: t.reshape(depth, batch, N_HEADS, HEAD_DIM, seq).transpose(0, 1, 4, 2, 3)
    head_s = lambda ts: jnp.stack(ts).reshape(depth, dec_batch, dec_seq, N_HEADS, HEAD_DIM)
    mem_shape = (depth, batch, MEM_LEN, MEM_HEADS, MEM_HEAD_DIM)
    return (y_prompt, y_sample, head_p(kv_p[0]), head_p(kv_p[1]), jnp.stack(cp_l),
            mem_k_p.reshape(mem_shape), mem_v_p.reshape(mem_shape),
            head_s(ks_l), head_s(vs_l), jnp.stack(cs_l))
```

```python
import functools

import jax
import jax.numpy as jnp
from jax import lax
from jax.experimental import pallas as pl
from jax.experimental.pallas import tpu as pltpu

F32 = jnp.float32
BF16 = jnp.bfloat16

D_MODEL = 1024
N_HEADS = 8
HEAD_DIM = 64
ATTN_W = N_HEADS * HEAD_DIM
ROT_DIM = HEAD_DIM // 4
ROPE_THETA = 500000.0
MOBA_BLOCK = 256
MOBA_TOP_K = 3
CONV_CH = 512
CONV_K = 31
CONV_HALO = CONV_K - 1
MEM_LEN = 256
MEM_HEADS = 4
MEM_HEAD_DIM = 128
MEM_W = MEM_HEADS * MEM_HEAD_DIM
PAGE_SIZE = 128
RMS_EPS = 1e-6
LN_EPS = 1e-5
ATTN_SCALE = HEAD_DIM ** -0.5
MEM_SCALE = MEM_HEAD_DIM ** -0.5

LANES = 128
SUBLANES = 8
LOG2E = 1.4426950408889634
MASK_BIAS = -1e30
MAX_FLOOR = -1e29
HEAD_ROWS = HEAD_DIM + 16
ROW_TILE = 256
LOGITS_AHEAD = 4
FUSED_PAGES = 32
PAGE_GROUP = 2
VALUE_GROUP = 8
SEQS_PER_STEP = 4
HALO_PAD = 32
VMEM_LIMIT = 56 * 1024 * 1024

_Q, _K, _V, _AG, _CV, _CG, _CGATE, _MQ, _MG, _MERGE = (
    0, 512, 1024, 1536, 2048, 2560, 3072, 3584, 4096, 4608)
N_IN = _MERGE + 3 * D_MODEL

_NT = (((1,), (1,)), ((), ()))


def _rms_norm(x, g):
    return x * lax.rsqrt(jnp.mean(x * x, axis=-1, keepdims=True) + RMS_EPS) * g


def _layer_norm(x, g, b):
    mu = jnp.mean(x, axis=-1, keepdims=True)
    xc = x - mu
    var = jnp.mean(xc * xc, axis=-1, keepdims=True)
    return xc * lax.rsqrt(var + LN_EPS) * g + b


def _rope(xc, cosf, sina, sinb):
    return (xc * cosf + pltpu.roll(xc, LANES - ROT_DIM // 2, 1) * sina
            + pltpu.roll(xc, ROT_DIM // 2, 1) * sinb)


def _rope_tables(pos):
    half = ROT_DIM // 2
    inv = ROPE_THETA ** (-jnp.arange(0, ROT_DIM, 2, dtype=F32) / ROT_DIM)
    ang = pos.astype(F32)[:, None] * inv[None, :]
    cos, sin = jnp.cos(ang), jnp.sin(ang)
    n = pos.shape[0]
    zeros_h = jnp.zeros((n, half), F32)
    rest0 = jnp.zeros((n, HEAD_DIM - ROT_DIM), F32)
    cosf = jnp.concatenate([cos, cos, jnp.ones((n, HEAD_DIM - ROT_DIM), F32)], axis=1)
    sina = jnp.concatenate([-sin, zeros_h, rest0], axis=1)
    sinb = jnp.concatenate([zeros_h, sin, rest0], axis=1)
    rep = LANES // HEAD_DIM
    return tuple(jnp.tile(t, (1, rep)) for t in (cosf, sina, sinb))


def _top_k_select(sc, valid, idx, n, axis=0):
    scm = jnp.where(valid, sc, -jnp.inf)
    rank = jnp.zeros(sc.shape, F32)
    for j in range(n):
        sj = scm[j:j + 1, :] if axis == 0 else scm[:, j:j + 1]
        beats = (sj > scm) | ((sj == scm) & (j < idx))
        rank = rank + jnp.where(beats, 1.0, 0.0)
    return valid & (rank < float(MOBA_TOP_K))


def _block_lane(h):
    return HEAD_DIM if h % 2 == 0 else 0


def _split_bf16(x):
    hi = x.astype(BF16)
    return hi, (x - hi.astype(F32)).astype(BF16)


def _merge_out(attn, sa, g0, part, x, wb0, wout):
    a = (attn * sa.astype(F32)).astype(BF16)
    pa = jnp.dot(a, wb0, preferred_element_type=F32)
    mix = (g0.astype(F32) * pa + part.astype(F32)).astype(BF16)
    return x + jnp.dot(mix, wout, preferred_element_type=F32)


def _mem_proj_kernel(mem_ref, wk_ref, wv_ref, mk_out, mv_out):
    mb = mem_ref[...].astype(BF16)
    mk_out[...] = jnp.dot(mb, wk_ref[...].astype(BF16), preferred_element_type=F32)
    mv_out[...] = jnp.dot(mb, wv_ref[...].astype(BF16), preferred_element_type=F32)


def _mem_proj(mem2d, w_mem_k, w_mem_v):
    depth = w_mem_k.shape[0]
    rows = mem2d.shape[0]
    w_spec = pl.BlockSpec((None, D_MODEL, MEM_W), lambda l: (l, 0, 0))
    o_spec = pl.BlockSpec((None, rows, MEM_W), lambda l: (l, 0, 0))
    return pl.pallas_call(
        _mem_proj_kernel,
        grid=(depth,),
        in_specs=[pl.BlockSpec((rows, D_MODEL), lambda l: (0, 0)), w_spec, w_spec],
        out_specs=[o_spec, o_spec],
        out_shape=[jax.ShapeDtypeStruct((depth, rows, MEM_W), F32)] * 2,
        compiler_params=pltpu.CompilerParams(dimension_semantics=("arbitrary",),
                                             vmem_limit_bytes=VMEM_LIMIT),
        name="mem_proj",
    )(mem2d, w_mem_k, w_mem_v)


def _proj_prompt_kernel(pt_ref, x_ref, g_ref, w_ref, cos_ref, sina_ref, sinb_ref, cw_ref, cb_ref, lng_ref, lnb_ref,
                        mk_ref, mv_ref, wb1_ref, wb2_ref, qs_ref, kn_ref, cache_ref, *refs,
                        tiles_per_seq, n_prev, layer, dec_seq, n_pages):
    if n_prev:
        ktp_ref, vtp_ref = refs[:2]
        refs = refs[2:]
    (q_out, kt_out, vt_out, kb_out, vtb_out, km_out, sa_out, g0_out, part_out, cst_out, p_out, l_out,
     ubuf, urot, page_buf, page_sem, logit_scr, qb_scr) = refs
    tm = ROW_TILE
    step = pl.program_id(0)
    tin = step % tiles_per_seq
    steps_per_seq = n_pages // FUSED_PAGES
    sb, chunk = step // steps_per_seq, step % steps_per_seq
    _page_copies(cache_ref, pt_ref, page_buf, page_sem, layer, sb, chunk, start=True)
    if n_prev:
        kt_out[0:n_prev] = ktp_ref[...]
        vt_out[0:n_prev] = vtp_ref[...]
    hb = _rms_norm(x_ref[...], g_ref[...]).astype(BF16)

    def seg(a, width):
        return jnp.dot(hb, w_ref[:, a:a + width], preferred_element_type=F32)

    cosf, sina, sinb = cos_ref[...], sina_ref[...], sinb_ref[...]
    zq = seg(_Q, ATTN_W)
    zk = seg(_K, ATTN_W)
    for c in range(ATTN_W // LANES):
        sl = slice(c * LANES, (c + 1) * LANES)
        q_out[:, sl] = _rope(zq[:, sl], cosf, sina, sinb)
        kr = _rope(zk[:, sl], cosf, sina, sinb)
        kt_out[n_prev, sl, :] = kr.T
        lane = lax.broadcasted_iota(jnp.int32, (tm, LANES), 1)
        for hh in range(2):
            onehot = jnp.where(lane == _block_lane(2 * c + hh) + tin, 1.0, 0.0)
            kb_out[:, (2 * c + hh) * LANES:(2 * c + hh + 1) * LANES] = jnp.where(
                lane // HEAD_DIM == hh, kr, onehot).astype(BF16)
        km_out[:, sl] = jnp.mean(kr, axis=0, keepdims=True)
    zvt = seg(_V, ATTN_W).T
    vt_out[n_prev] = zvt
    for h in range(N_HEADS):
        vtb_out[h * HEAD_ROWS:h * HEAD_ROWS + HEAD_DIM, :] = zvt[h * HEAD_DIM:(h + 1) * HEAD_DIM, :].astype(BF16)
        vtb_out[h * HEAD_ROWS + HEAD_DIM:(h + 1) * HEAD_ROWS, :] = jnp.ones((HEAD_ROWS - HEAD_DIM, tm), BF16)
    sa_out[...] = jax.nn.silu(seg(_AG, ATTN_W)).astype(BF16)

    u = seg(_CV, CONV_CH) * jax.nn.sigmoid(seg(_CG, CONV_CH))

    @pl.when(tin == 0)
    def _():
        ubuf[0:HALO_PAD, :] = jnp.zeros((HALO_PAD, CONV_CH), F32)

    @pl.when(tin != 0)
    def _():
        ubuf[HALO_PAD - CONV_HALO:HALO_PAD, :] = ubuf[tm + HALO_PAD - CONV_HALO:tm + HALO_PAD, :]

    ubuf[HALO_PAD:HALO_PAD + tm, :] = u
    nrot = tm + HALO_PAD - SUBLANES
    for r in range(1, SUBLANES):
        urot[r - 1] = ubuf[r:r + nrot, :]
    cst_out[...] = ubuf[tm + HALO_PAD - CONV_HALO:tm + HALO_PAD, :]
    conv = jnp.broadcast_to(cb_ref[...], (tm, CONV_CH))
    for t in range(CONV_K):
        a, r = divmod(HALO_PAD - CONV_HALO + t, SUBLANES)
        rows = slice(a * SUBLANES, a * SUBLANES + tm)
        conv = conv + cw_ref[t:t + 1, :] * (ubuf[rows, :] if r == 0 else urot[r - 1, rows, :])
    cbr = jax.nn.silu(_layer_norm(conv, lng_ref[...], lnb_ref[...])) * jax.nn.silu(seg(_CGATE, CONV_CH))

    zmq = seg(_MQ, MEM_W)
    mparts = []
    for hd in range(MEM_HEADS):
        sl = slice(hd * MEM_HEAD_DIM, (hd + 1) * MEM_HEAD_DIM)
        s = lax.dot_general(zmq[:, sl].astype(BF16), mk_ref[:, sl].astype(BF16), _NT,
                            preferred_element_type=F32) * MEM_SCALE
        p = jnp.exp(s - jnp.max(s, axis=-1, keepdims=True))
        o = jnp.dot(p.astype(BF16), mv_ref[:, sl].astype(BF16), preferred_element_type=F32)
        mparts.append(o / jnp.sum(p, axis=-1, keepdims=True))
    mbr = jnp.concatenate(mparts, axis=1) * jax.nn.silu(seg(_MG, MEM_W))

    pc = jnp.dot(cbr.astype(BF16), wb1_ref[...], preferred_element_type=F32)
    pm = jnp.dot(mbr.astype(BF16), wb2_ref[...], preferred_element_type=F32)
    g0_out[...] = jax.nn.sigmoid(seg(_MERGE, D_MODEL)).astype(BF16)
    part_out[...] = (jax.nn.sigmoid(seg(_MERGE + D_MODEL, D_MODEL)) * pc
                     + jax.nn.sigmoid(seg(_MERGE + 2 * D_MODEL, D_MODEL)) * pm).astype(BF16)

    _page_copies(cache_ref, pt_ref, page_buf, page_sem, layer, sb, chunk, start=False)
    _sample_logits_pages(qs_ref, page_buf, logit_scr, qb_scr, chunk, dec_seq=dec_seq)
    _sample_logits_finish(kn_ref, logit_scr, qb_scr, p_out, l_out, chunk, dec_seq=dec_seq, n_pages=n_pages)


def _const_spec(shape, ngrid=1):
    zeros = (0,) * len(shape)
    if ngrid == 1:
        return pl.BlockSpec(shape, lambda i: zeros)
    return pl.BlockSpec(shape, lambda i, j: zeros)


def _proj_prompt(page_table, x2d, g, w_in_b, tabs, cw, cb, lng, lnb, mk, mv, w_br_b, q_s, k_s, cache_kt,
                 kv_prev, layer, batch, seq, dec_seq):
    tm = ROW_TILE
    n = batch * seq
    tps = seq // tm
    nt = n // tm
    dec_batch, n_pages = page_table.shape
    sps = n_pages // FUSED_PAGES
    assert nt == dec_batch * sps
    rows_s = N_HEADS * dec_seq
    n_prev = 0 if kv_prev is None else kv_prev[0].shape[0]
    const = lambda shape: pl.BlockSpec(shape, lambda t, pt: (0,) * len(shape))
    row = lambda w: pl.BlockSpec((tm, w), lambda t, pt: (t, 0))
    tab = pl.BlockSpec((tm, LANES), lambda t, pt: (t % tps, 0))
    mem = pl.BlockSpec((None, MEM_LEN, MEM_W), lambda t, pt: (layer, t // tps, 0))
    wbr = lambda br: pl.BlockSpec((None, None, ATTN_W, D_MODEL), lambda t, pt: (layer, br, 0, 0))
    seq_rows = pl.BlockSpec((dec_seq, ATTN_W), lambda t, pt: (t // sps, 0))
    in_specs = [
        row(D_MODEL), const((1, D_MODEL)),
        pl.BlockSpec((None, D_MODEL, N_IN), lambda t, pt: (layer, 0, 0), pipeline_mode=pl.Buffered(1)),
        tab, tab, tab,
        const((CONV_K, CONV_CH)), const((1, CONV_CH)), const((1, CONV_CH)), const((1, CONV_CH)),
        mem, mem, wbr(1), wbr(2),
        seq_rows, seq_rows, pl.BlockSpec(memory_space=pl.ANY),
    ]
    args = [page_table, x2d, g, w_in_b, *tabs, cw, cb, lng, lnb, mk, mv, w_br_b, w_br_b, q_s, k_s, cache_kt]
    if n_prev:
        prev = pl.BlockSpec((n_prev, None, ATTN_W, tm), lambda t, pt: (0, t // tps, 0, t % tps))
        in_specs += [prev, prev]
        args += list(kv_prev)
    out_shape = [
        jax.ShapeDtypeStruct((n, ATTN_W), F32),
        jax.ShapeDtypeStruct((n_prev + 1, batch, ATTN_W, seq), F32),
        jax.ShapeDtypeStruct((n_prev + 1, batch, ATTN_W, seq), F32),
        jax.ShapeDtypeStruct((nt, tm, N_HEADS * LANES), BF16),
        jax.ShapeDtypeStruct((nt, N_HEADS * HEAD_ROWS, tm), BF16),
        jax.ShapeDtypeStruct((nt, 1, ATTN_W), F32),
        jax.ShapeDtypeStruct((n, ATTN_W), BF16),
        jax.ShapeDtypeStruct((n, D_MODEL), BF16),
        jax.ShapeDtypeStruct((n, D_MODEL), BF16),
        jax.ShapeDtypeStruct((batch, CONV_HALO, CONV_CH), F32),
        jax.ShapeDtypeStruct((dec_batch, n_pages + 1, rows_s, PAGE_SIZE), BF16),
        jax.ShapeDtypeStruct((dec_batch, rows_s, LANES), F32),
    ]
    seq_t = pl.BlockSpec((n_prev + 1, None, ATTN_W, tm), lambda t, pt: (0, t // tps, 0, t % tps))
    out_specs = [
        row(ATTN_W), seq_t, seq_t,
        pl.BlockSpec((None, tm, N_HEADS * LANES), lambda t, pt: (t, 0, 0)),
        pl.BlockSpec((None, N_HEADS * HEAD_ROWS, tm), lambda t, pt: (t, 0, 0)),
        pl.BlockSpec((None, 1, ATTN_W), lambda t, pt: (t, 0, 0)),
        row(ATTN_W), row(D_MODEL), row(D_MODEL),
        pl.BlockSpec((None, CONV_HALO, CONV_CH), lambda t, pt: (t // tps, 0, 0)),
        pl.BlockSpec((None, n_pages + 1, rows_s, PAGE_SIZE), lambda t, pt: (t // sps, 0, 0, 0)),
        pl.BlockSpec((None, rows_s, LANES), lambda t, pt: (t // sps, 0, 0)),
    ]
    return pl.pallas_call(
        functools.partial(_proj_prompt_kernel, tiles_per_seq=tps, n_prev=n_prev, layer=layer,
                          dec_seq=dec_seq, n_pages=n_pages),
        grid_spec=pltpu.PrefetchScalarGridSpec(
            num_scalar_prefetch=1,
            grid=(nt,),
            in_specs=in_specs,
            out_specs=out_specs,
            scratch_shapes=[pltpu.VMEM((tm + HALO_PAD, CONV_CH), F32),
                            pltpu.VMEM((SUBLANES - 1, tm + HALO_PAD - SUBLANES, CONV_CH), F32),
                            pltpu.VMEM((FUSED_PAGES, N_HEADS, HEAD_DIM, PAGE_SIZE), F32),
                            pltpu.SemaphoreType.DMA((FUSED_PAGES,)),
                            pltpu.VMEM((n_pages, rows_s, PAGE_SIZE), F32),
                            pltpu.VMEM((rows_s, ATTN_W), BF16)]),
        out_shape=out_shape,
        compiler_params=pltpu.CompilerParams(dimension_semantics=("arbitrary",),
                                             vmem_limit_bytes=VMEM_LIMIT),
        name="proj_prompt",
    )(*args)


def _attn_prompt_kernel(pt_ref, q_ref, kb_ref, vt_ref, km_ref, sa_ref, g0_ref, part_ref, x_ref, wb0_ref, wout_ref,
                        gf_ref, ps_ref, pns_ref, ls_ref, vn_ref, cache_ref, o_ref, so_ref,
                        sc_scr, sel_scr, qb_scr, m_scr, acc_scr, page_buf, page_sem, sacc_scr,
                        *, final, nblk, layer, dec_seq, n_pages):
    tq = ROW_TILE
    i = pl.program_id(1)
    step = pl.program_id(0) * nblk + i
    steps_per_seq = n_pages // FUSED_PAGES
    sb, chunk = step // steps_per_seq, step % steps_per_seq
    _page_copies(cache_ref, pt_ref, page_buf, page_sem, layer, sb, chunk, start=True)
    blk_idx = lax.broadcasted_iota(jnp.int32, (nblk, tq), 0)
    valid = blk_idx < i
    lane = lax.broadcasted_iota(jnp.int32, (tq, LANES), 1)
    causal = (lax.broadcasted_iota(jnp.int32, (MOBA_BLOCK, tq), 0)
              <= lax.broadcasted_iota(jnp.int32, (MOBA_BLOCK, tq), 1))
    pair = lambda h: slice((h // 2) * LANES, (h // 2 + 1) * LANES)
    col = lambda h: slice(h * LANES, (h + 1) * LANES)
    rows = lambda h: slice(h * HEAD_ROWS, (h + 1) * HEAD_ROWS)
    bcast = lambda r: jnp.broadcast_to(r, (SUBLANES, tq))
    own_half = lambda h: jnp.where((lane // HEAD_DIM) == h % 2, q_ref[:, pair(h)], 0.0)

    for h in range(N_HEADS):
        qm = own_half(h)
        q_hi, q_lo = _split_bf16(qm)
        km_hi, km_lo = _split_bf16(km_ref[:, pair(h)])
        sc = lax.dot_general(jnp.concatenate([km_hi, km_hi, km_lo], axis=1),
                             jnp.concatenate([q_hi, q_lo, q_hi], axis=1), _NT,
                             preferred_element_type=F32)
        sc_scr[h] = jnp.where(valid, sc, -jnp.inf)
        sel_scr[h] = jnp.zeros((nblk, tq), F32)
        m_scr[h] = jnp.full((SUBLANES, tq), MAX_FLOOR, F32)
    acc_scr[...] = jnp.zeros((N_HEADS * HEAD_ROWS, tq), F32)

    def rank_body(j, carry):
        for h in range(N_HEADS):
            scm = sc_scr[h]
            sj = sc_scr[h, pl.ds(j, 1), :]
            before = (sj > scm) | ((sj == scm) & (j < blk_idx))
            sel_scr[h] = sel_scr[h] + jnp.where(before, 1.0, 0.0)
        return carry

    lax.fori_loop(0, i, rank_body, 0)
    for h in range(N_HEADS):
        keep = (valid & (sel_scr[h] < float(MOBA_TOP_K))) | (blk_idx == i)
        pieces = [jnp.where(keep, 0.0, MASK_BIAS)]
        if _block_lane(h):
            pieces.insert(0, jnp.zeros((_block_lane(h), tq), F32))
        pieces.append(jnp.zeros((LANES - _block_lane(h) - nblk, tq), F32))
        bias = jnp.concatenate(pieces, axis=0).T
        qb_scr[h] = (own_half(h) * (ATTN_SCALE * LOG2E) + bias).astype(BF16)

    def logits(j, h):
        return lax.dot_general(kb_ref[j, :, col(h)], qb_scr[h], _NT, preferred_element_type=F32)

    def softmax_step(j, h, s):
        m_old = m_scr[h, 0:1, :]
        m_new = jnp.maximum(m_old, jnp.max(s, axis=0, keepdims=True))
        alpha = jnp.exp2(m_old - m_new)
        p = jnp.exp2(s - m_new)
        m_scr[h] = bcast(m_new)
        acc_scr[rows(h), :] = alpha * acc_scr[rows(h), :] + jnp.dot(
            vt_ref[j, rows(h), :], p.astype(BF16), preferred_element_type=F32)

    def body(j, ahead):
        ahead = list(ahead)
        for h in range(N_HEADS):
            s = ahead.pop(0)
            nh = h + LOGITS_AHEAD
            ahead.append(logits(j, nh) if nh < N_HEADS else logits(j + 1, nh - N_HEADS))
            softmax_step(j, h, s)
        return tuple(ahead)

    ahead = list(lax.fori_loop(0, i, body, tuple(logits(0, h) for h in range(LOGITS_AHEAD))))
    for h in range(N_HEADS):
        s = ahead.pop(0)
        if h + LOGITS_AHEAD < N_HEADS:
            ahead.append(logits(i, h + LOGITS_AHEAD))
        softmax_step(i, h, jnp.where(causal, s, -jnp.inf))
    outs = []
    for h in range(N_HEADS):
        blk = acc_scr[rows(h), :]
        outs.append(blk[0:HEAD_DIM, :] / blk[HEAD_DIM:HEAD_DIM + 1, :])

    attn = jnp.concatenate(outs, axis=0).T
    xn = _merge_out(attn, sa_ref[...], g0_ref[...], part_ref[...], x_ref[...], wb0_ref[...], wout_ref[...])
    if final:
        xn = _rms_norm(xn, gf_ref[...])
    o_ref[...] = xn

    _page_copies(cache_ref, pt_ref, page_buf, page_sem, layer, sb, chunk, start=False)
    _sample_values_step(ps_ref, pns_ref, ls_ref, vn_ref, page_buf, sacc_scr, so_ref, chunk,
                        dec_seq=dec_seq, n_chunks=steps_per_seq)


def _attn_prompt(page_table, q, kb, vt, km, sa, g0, part, x2d, w_br_b, w_out_b, gf, p_s, l_s, v_s, cache_vt,
                 layer, batch, seq, dec_seq, final):
    tq = ROW_TILE
    nblk = seq // MOBA_BLOCK
    dec_batch, n_pages = page_table.shape
    sps = n_pages // FUSED_PAGES
    assert batch * nblk == dec_batch * sps
    rows_s = N_HEADS * dec_seq
    sstep = lambda b, i: (b * nblk + i) // sps
    row = lambda w: pl.BlockSpec((tq, w), lambda b, i, pt: (b * nblk + i, 0))
    seq_rows = pl.BlockSpec((dec_seq, ATTN_W), lambda b, i, pt: (sstep(b, i), 0))
    in_specs = [
        row(ATTN_W),
        pl.BlockSpec((nblk, MOBA_BLOCK, N_HEADS * LANES), lambda b, i, pt: (b, 0, 0)),
        pl.BlockSpec((nblk, N_HEADS * HEAD_ROWS, MOBA_BLOCK), lambda b, i, pt: (b, 0, 0)),
        pl.BlockSpec((None, nblk, ATTN_W), lambda b, i, pt: (b, 0, 0)),
        row(ATTN_W), row(D_MODEL), row(D_MODEL), row(D_MODEL),
        pl.BlockSpec((None, None, ATTN_W, D_MODEL), lambda b, i, pt: (layer, 0, 0, 0)),
        pl.BlockSpec((None, D_MODEL, D_MODEL), lambda b, i, pt: (layer, 0, 0)),
        pl.BlockSpec((1, D_MODEL), lambda b, i, pt: (0, 0)),
        pl.BlockSpec((None, FUSED_PAGES, rows_s, PAGE_SIZE),
                     lambda b, i, pt: (sstep(b, i), (b * nblk + i) % sps, 0, 0)),
        pl.BlockSpec((None, None, rows_s, PAGE_SIZE), lambda b, i, pt: (sstep(b, i), n_pages, 0, 0)),
        pl.BlockSpec((None, rows_s, LANES), lambda b, i, pt: (sstep(b, i), 0, 0)),
        seq_rows, pl.BlockSpec(memory_space=pl.ANY),
    ]
    return pl.pallas_call(
        functools.partial(_attn_prompt_kernel, final=final, nblk=nblk, layer=layer, dec_seq=dec_seq,
                          n_pages=n_pages),
        grid_spec=pltpu.PrefetchScalarGridSpec(
            num_scalar_prefetch=1,
            grid=(batch, nblk),
            in_specs=in_specs,
            out_specs=[row(D_MODEL), seq_rows],
            scratch_shapes=[pltpu.VMEM((N_HEADS, nblk, tq), F32),
                            pltpu.VMEM((N_HEADS, nblk, tq), F32),
                            pltpu.VMEM((N_HEADS, tq, LANES), BF16),
                            pltpu.VMEM((N_HEADS, SUBLANES, tq), F32),
                            pltpu.VMEM((N_HEADS * HEAD_ROWS, tq), F32),
                            pltpu.VMEM((FUSED_PAGES, N_HEADS, HEAD_DIM, PAGE_SIZE), F32),
                            pltpu.SemaphoreType.DMA((FUSED_PAGES,)),
                            pltpu.VMEM((rows_s, ATTN_W), F32)]),
        out_shape=[jax.ShapeDtypeStruct((batch * seq, D_MODEL), F32),
                   jax.ShapeDtypeStruct((dec_batch * dec_seq, ATTN_W), F32)],
        compiler_params=pltpu.CompilerParams(dimension_semantics=("arbitrary", "arbitrary"),
                                             vmem_limit_bytes=VMEM_LIMIT),
        name="attn_prompt",
    )(page_table, q, kb, vt, km, sa, g0, part, x2d, w_br_b, w_out_b, gf, p_s, p_s, l_s, v_s, cache_vt)


def _proj_sample_kernel(x_ref, g_ref, w_ref, cos_ref, sina_ref, sinb_ref, cw_ref, cb_ref, lng_ref, lnb_ref,
                        st_ref, mk_ref, mv_ref, wb1_ref, wb2_ref,
                        q_out, k_out, v_out, sa_out, g0_out, part_out, cst_out,
                        u_scr, cgate_scr, mq_scr, mgate_scr, g12_scr, c_scr, m_scr, full_scr, *, dec_seq):
    b = pl.program_id(0)
    nb = pl.num_programs(0)

    @pl.when(b == 0)
    def _():
        hb = _rms_norm(x_ref[...], g_ref[...]).astype(BF16)

        def seg(a, width):
            return jnp.dot(hb, w_ref[:, a:a + width], preferred_element_type=F32)

        cosf, sina, sinb = cos_ref[...], sina_ref[...], sinb_ref[...]
        zq = seg(_Q, ATTN_W)
        zk = seg(_K, ATTN_W)
        for c in range(ATTN_W // LANES):
            sl = slice(c * LANES, (c + 1) * LANES)
            q_out[:, sl] = _rope(zq[:, sl], cosf, sina, sinb)
            k_out[:, sl] = _rope(zk[:, sl], cosf, sina, sinb)
        v_out[...] = seg(_V, ATTN_W)
        sa_out[...] = jax.nn.silu(seg(_AG, ATTN_W))
        u_scr[...] = seg(_CV, CONV_CH) * jax.nn.sigmoid(seg(_CG, CONV_CH))
        cgate_scr[...] = jax.nn.silu(seg(_CGATE, CONV_CH))
        mq_scr[...] = seg(_MQ, MEM_W)
        mgate_scr[...] = jax.nn.silu(seg(_MG, MEM_W))
        g0_out[...] = jax.nn.sigmoid(seg(_MERGE, D_MODEL))
        g12_scr[:, 0:D_MODEL] = jax.nn.sigmoid(seg(_MERGE + D_MODEL, D_MODEL))
        g12_scr[:, D_MODEL:2 * D_MODEL] = jax.nn.sigmoid(seg(_MERGE + 2 * D_MODEL, D_MODEL))

    for sq in range(SEQS_PER_STEP):
        r0 = pl.multiple_of((b * SEQS_PER_STEP + sq) * dec_seq, dec_seq)
        rows = pl.ds(r0, dec_seq)

        full_scr[sq, 0:CONV_HALO, :] = st_ref[sq]
        full_scr[sq, CONV_HALO:CONV_HALO + dec_seq, :] = u_scr[rows, :]
        conv = jnp.broadcast_to(cb_ref[...], (dec_seq, CONV_CH))
        for t in range(CONV_K):
            conv = conv + cw_ref[t:t + 1, :] * full_scr[sq, t:t + dec_seq, :]
        cst_out[sq] = full_scr[sq, dec_seq:dec_seq + CONV_HALO, :]
        c_scr[rows, :] = jax.nn.silu(_layer_norm(conv, lng_ref[...], lnb_ref[...])) * cgate_scr[rows, :]

        mq = mq_scr[rows, :]
        mparts = []
        for hd in range(MEM_HEADS):
            sl = slice(hd * MEM_HEAD_DIM, (hd + 1) * MEM_HEAD_DIM)
            s = lax.dot_general(mq[:, sl], mk_ref[sq, :, hd, :], _NT, preferred_element_type=F32) * MEM_SCALE
            p = jnp.exp(s - jnp.max(s, axis=-1, keepdims=True))
            o = jnp.dot(p, mv_ref[sq, :, hd, :], preferred_element_type=F32)
            mparts.append(o / jnp.sum(p, axis=-1, keepdims=True))
        m_scr[rows, :] = jnp.concatenate(mparts, axis=1) * mgate_scr[rows, :]

    @pl.when(b == nb - 1)
    def _():
        pc = jnp.dot(c_scr[...].astype(BF16), wb1_ref[...], preferred_element_type=F32)
        pm = jnp.dot(m_scr[...].astype(BF16), wb2_ref[...], preferred_element_type=F32)
        part_out[...] = g12_scr[:, 0:D_MODEL] * pc + g12_scr[:, D_MODEL:2 * D_MODEL] * pm


def _proj_sample(x2d, g, w_in_b, tabs, cw, cb, lng, lnb, state_conv, cache_mem_k, cache_mem_v, w_br_b,
                 layer, dec_batch, dec_seq):
    n = dec_batch * dec_seq
    full = lambda w: _const_spec((n, w))
    assert dec_batch % SEQS_PER_STEP == 0
    mem = pl.BlockSpec((None, SEQS_PER_STEP, MEM_LEN, MEM_HEADS, MEM_HEAD_DIM), lambda b: (layer, b, 0, 0, 0))
    wbr = lambda br: pl.BlockSpec((None, None, ATTN_W, D_MODEL), lambda b: (layer, br, 0, 0))
    in_specs = [
        full(D_MODEL), _const_spec((1, D_MODEL)),
        pl.BlockSpec((None, D_MODEL, N_IN), lambda b: (layer, 0, 0), pipeline_mode=pl.Buffered(1)),
        full(LANES), full(LANES), full(LANES),
        _const_spec((CONV_K, CONV_CH)), _const_spec((1, CONV_CH)), _const_spec((1, CONV_CH)),
        _const_spec((1, CONV_CH)),
        pl.BlockSpec((None, SEQS_PER_STEP, CONV_HALO, CONV_CH), lambda b: (layer, b, 0, 0)),
        mem, mem, wbr(1), wbr(2),
    ]
    out_shape = [jax.ShapeDtypeStruct((n, ATTN_W), F32)] * 4 + [
        jax.ShapeDtypeStruct((n, D_MODEL), F32), jax.ShapeDtypeStruct((n, D_MODEL), F32),
        jax.ShapeDtypeStruct((dec_batch, CONV_HALO, CONV_CH), F32)]
    out_specs = [full(ATTN_W)] * 4 + [full(D_MODEL), full(D_MODEL),
                                      pl.BlockSpec((SEQS_PER_STEP, CONV_HALO, CONV_CH), lambda b: (b, 0, 0))]
    scr = lambda w: pltpu.VMEM((n, w), F32)
    return pl.pallas_call(
        functools.partial(_proj_sample_kernel, dec_seq=dec_seq),
        grid=(dec_batch // SEQS_PER_STEP,),
        in_specs=in_specs,
        out_specs=out_specs,
        out_shape=out_shape,
        scratch_shapes=[scr(CONV_CH), scr(CONV_CH), scr(MEM_W), scr(MEM_W), scr(2 * D_MODEL),
                        scr(CONV_CH), scr(MEM_W),
                        pltpu.VMEM((SEQS_PER_STEP, CONV_HALO + dec_seq + 2, CONV_CH), F32)],
        compiler_params=pltpu.CompilerParams(dimension_semantics=("arbitrary",),
                                             vmem_limit_bytes=VMEM_LIMIT),
        name="proj_sample",
    )(x2d, g, w_in_b, *tabs, cw, cb, lng, lnb, state_conv, cache_mem_k, cache_mem_v, w_br_b, w_br_b)


def _page_copies(cache_ref, pt_ref, page_buf, page_sem, layer, sb, chunk, *, start):
    for r in range(FUSED_PAGES):
        page = pt_ref[sb, chunk * FUSED_PAGES + r]
        copy = pltpu.make_async_copy(cache_ref.at[layer, page], page_buf.at[r], page_sem.at[r])
        if start:
            copy.start()
        else:
            copy.wait()


def _head_rows(x, dec_seq):
    rows = N_HEADS * dec_seq
    tiled = jnp.concatenate([x] * N_HEADS, axis=0)
    row_h = lax.broadcasted_iota(jnp.int32, (rows, ATTN_W), 0) // dec_seq
    lane_h = lax.broadcasted_iota(jnp.int32, (rows, ATTN_W), 1) // HEAD_DIM
    return jnp.where(row_h == lane_h, tiled, 0.0)


def _sample_logits_pages(q_ref, page_buf, logit_scr, qb_scr, c, *, dec_seq):
    @pl.when(c == 0)
    def _():
        qb_scr[...] = (_head_rows(q_ref[...], dec_seq) * ATTN_SCALE).astype(BF16)

    qb = qb_scr[...]
    for r0 in range(0, FUSED_PAGES, PAGE_GROUP):
        kt = jnp.concatenate([page_buf[r0 + g].reshape(ATTN_W, PAGE_SIZE).astype(BF16)
                              for g in range(PAGE_GROUP)], axis=1)
        lg = jnp.dot(qb, kt, preferred_element_type=F32)
        for g in range(PAGE_GROUP):
            logit_scr[c * FUSED_PAGES + r0 + g] = lg[:, g * PAGE_SIZE:(g + 1) * PAGE_SIZE]


def _sample_logits_finish(kn_ref, logit_scr, qb_scr, p_out, l_out, c, *, dec_seq, n_pages):
    nc = n_pages // FUSED_PAGES
    rows = N_HEADS * dec_seq
    pages_per_blk = MOBA_BLOCK // PAGE_SIZE
    nblk = n_pages // pages_per_blk

    @pl.when(c == nc - 1)
    def _():
        qb = qb_scr[...]
        blk_idx = lax.broadcasted_iota(jnp.int32, (rows, nblk), 1)
        sc = jnp.zeros((rows, nblk), F32)
        for j in range(nblk):
            blk = logit_scr[j * pages_per_blk]
            for pp in range(1, pages_per_blk):
                blk = blk + logit_scr[j * pages_per_blk + pp]
            sc = jnp.where(blk_idx == j, jnp.sum(blk, axis=-1, keepdims=True), sc)
        sel = _top_k_select(sc, blk_idx >= 0, blk_idx, nblk, axis=1)
        sel_t = jnp.where(sel, 1.0, 0.0)

        kn_page = jnp.concatenate([kn_ref[...], jnp.zeros((PAGE_SIZE - dec_seq, ATTN_W), F32)], axis=0)
        ln = lax.dot_general(qb, kn_page.astype(BF16), _NT, preferred_element_type=F32)
        key_i = lax.broadcasted_iota(jnp.int32, (rows, PAGE_SIZE), 1)
        qry_i = lax.broadcasted_iota(jnp.int32, (rows, PAGE_SIZE), 0) % dec_seq
        ln = jnp.where(key_i <= qry_i, ln, -jnp.inf)

        mx = ln
        for p in range(n_pages):
            j = p // pages_per_blk
            lp = jnp.where(sel_t[:, j:j + 1] > 0.0, logit_scr[p], -jnp.inf)
            logit_scr[p] = lp
            mx = jnp.maximum(mx, lp)
        m = jnp.max(mx, axis=-1, keepdims=True)
        pn = jnp.exp(ln - m)
        p_out[n_pages] = pn.astype(BF16)
        lsum = pn
        for p in range(n_pages):
            pp = jnp.exp(logit_scr[p] - m)
            p_out[p] = pp.astype(BF16)
            lsum = lsum + pp
        l_out[...] = jnp.broadcast_to(jnp.sum(lsum, axis=-1, keepdims=True), (rows, LANES))


def _sample_values_step(p_ref, pn_ref, l_ref, vn_ref, page_buf, acc_scr, o_ref, c, *, dec_seq, n_chunks):
    @pl.when(c == 0)
    def _():
        vn_page = jnp.concatenate([vn_ref[...], jnp.zeros((PAGE_SIZE - dec_seq, ATTN_W), F32)], axis=0)
        acc_scr[...] = jnp.dot(pn_ref[...], vn_page.astype(BF16), preferred_element_type=F32)

    acc = acc_scr[...]
    for r0 in range(0, FUSED_PAGES, VALUE_GROUP):
        vt = jnp.concatenate([page_buf[r0 + g].reshape(ATTN_W, PAGE_SIZE).astype(BF16)
                              for g in range(VALUE_GROUP)], axis=1)
        pw = jnp.concatenate([p_ref[r0 + g] for g in range(VALUE_GROUP)], axis=1)
        acc = acc + lax.dot_general(pw, vt, _NT, preferred_element_type=F32)
    acc_scr[...] = acc

    @pl.when(c == n_chunks - 1)
    def _():
        res = acc / jnp.concatenate([l_ref[...]] * (ATTN_W // LANES), axis=1)
        lane_h = lax.broadcasted_iota(jnp.int32, (dec_seq, ATTN_W), 1) // HEAD_DIM
        out = jnp.zeros((dec_seq, ATTN_W), F32)
        for h in range(N_HEADS):
            out = out + jnp.where(lane_h == h, res[h * dec_seq:(h + 1) * dec_seq, :], 0.0)
        o_ref[...] = out


def _merge_sample_kernel(a_ref, sa_ref, g0_ref, part_ref, x_ref, wb0_ref, wout_ref, gf_ref, o_ref, *, final):
    xn = _merge_out(a_ref[...], sa_ref[...], g0_ref[...], part_ref[...], x_ref[...], wb0_ref[...],
                    wout_ref[...])
    if final:
        xn = _rms_norm(xn, gf_ref[...])
    o_ref[...] = xn


def _merge_sample(attn, sa, g0, part, x2d, w_br_b, w_out_b, gf, layer, final):
    n = x2d.shape[0]
    full = lambda w: _const_spec((n, w))
    return pl.pallas_call(
        functools.partial(_merge_sample_kernel, final=final),
        grid=(1,),
        in_specs=[full(ATTN_W), full(ATTN_W), full(D_MODEL), full(D_MODEL), full(D_MODEL),
                  pl.BlockSpec((None, None, ATTN_W, D_MODEL), lambda i: (layer, 0, 0, 0)),
                  pl.BlockSpec((None, D_MODEL, D_MODEL), lambda i: (layer, 0, 0)), _const_spec((1, D_MODEL))],
        out_specs=full(D_MODEL),
        out_shape=jax.ShapeDtypeStruct((n, D_MODEL), F32),
        compiler_params=pltpu.CompilerParams(dimension_semantics=("arbitrary",),
                                             vmem_limit_bytes=VMEM_LIMIT),
        name="merge_sample",
    )(attn, sa, g0, part, x2d, w_br_b, w_out_b, gf)


def kernel(x_prompt, x_sample, cache_k, cache_v, cache_mem_k, cache_mem_v, state_conv, page_table, mem_prompt,
           g_norm, w_in, conv_w, conv_b, ln_g, ln_b, w_mem_k, w_mem_v, w_branch, w_out, g_final):
    batch, seq, _ = x_prompt.shape
    dec_batch, dec_seq, _ = x_sample.shape
    depth = w_in.shape[0]
    past_len = page_table.shape[1] * PAGE_SIZE
    assert seq % ROW_TILE == 0 and ROW_TILE == MOBA_BLOCK
    assert dec_batch * dec_seq == ROW_TILE and dec_seq == 8 and past_len % MOBA_BLOCK == 0
    assert page_table.shape[1] % FUSED_PAGES == 0

    w_in_b = w_in.astype(BF16)
    w_br_b = w_branch.astype(BF16)
    w_out_b = w_out.astype(BF16)
    gf = g_final.reshape(1, D_MODEL)

    mem_k_p, mem_v_p = _mem_proj(mem_prompt.reshape(batch * MEM_LEN, D_MODEL), w_mem_k, w_mem_v)

    tabs_p = _rope_tables(jnp.arange(seq, dtype=jnp.int32))
    tabs_s = tuple(jnp.tile(t, (dec_batch, 1))
                   for t in _rope_tables(past_len + jnp.arange(dec_seq, dtype=jnp.int32)))

    cache_kt = cache_k.transpose(0, 1, 3, 4, 2)
    cache_vt = cache_v.transpose(0, 1, 3, 4, 2)

    xp = x_prompt.reshape(batch * seq, D_MODEL)
    xs = x_sample.reshape(dec_batch * dec_seq, D_MODEL)
    kv_p = None
    cp_l, ks_l, vs_l, cs_l = [], [], [], []
    for l in range(depth):
        g = g_norm[l].reshape(1, D_MODEL)
        cw, cb = conv_w[l], conv_b[l].reshape(1, CONV_CH)
        lng, lnb = ln_g[l].reshape(1, CONV_CH), ln_b[l].reshape(1, CONV_CH)
        final = l == depth - 1

        qs, k_s, v_s, sas, g0s, parts, csts = _proj_sample(
            xs, g, w_in_b, tabs_s, cw, cb, lng, lnb, state_conv, cache_mem_k, cache_mem_v, w_br_b,
            l, dec_batch, dec_seq)
        q, kt, vt, kb, vtb, km, sa, g0, part, cst, p_s, l_s = _proj_prompt(
            page_table, xp, g, w_in_b, tabs_p, cw, cb, lng, lnb, mem_k_p, mem_v_p, w_br_b, qs, k_s, cache_kt,
            kv_p, l, batch, seq, dec_seq)
        xp, attn_s = _attn_prompt(
            page_table, q, kb, vtb, km.reshape(batch, seq // MOBA_BLOCK, ATTN_W), sa, g0, part, xp,
            w_br_b, w_out_b, gf, p_s, l_s, v_s, cache_vt, l, batch, seq, dec_seq, final)
        kv_p = (kt, vt)
        cp_l.append(cst)
        xs = _merge_sample(attn_s, sas, g0s, parts, xs, w_br_b, w_out_b, gf, l, final)
        ks_l.append(k_s); vs_l.append(v_s); cs_l.append(csts)

    y_prompt = xp.reshape(batch, seq, D_MODEL)
    y_sample = xs.reshape(dec_batch, dec_seq, D_MODEL)
    head_p = lambda t: t.reshape(depth, batch, N_HEADS, HEAD_DIM, seq).transpose(0, 1, 4, 2, 3)
    head_s = lambda ts: jnp.stack(ts).reshape(depth, dec_batch, dec_seq, N_HEADS, HEAD_DIM)
    mem_shape = (depth, batch, MEM_LEN, MEM_HEADS, MEM_HEAD_DIM)
    return (y_prompt, y_sample, head_p(kv_p[0]), head_p(kv_p[1]), jnp.stack(cp_l),
            mem_k_p.reshape(mem_shape), mem_v_p.reshape(mem_shape),
            head_s(ks_l), head_s(vs_l), jnp.stack(cs_l))
```
